```python
import jax
import jax.numpy as jnp
from jax import lax
import numpy as np

D_MODEL = 1024
BATCH = 32
SEQ = 256
DEPTH = 4
DEC_BATCH = 2
DEC_SEQ = 2048
PAST_LEN = 512

GRID_W = 64
N_MIXERS = 4
Q_BLOCK = 128
N_MOD = 9
MACARON_WEIGHT = 0.5
NORM_EPS = 1e-6
LN_EPS = 1e-5
ROPE_THETA = 10000.0
D_FF = ((8 * D_MODEL // 3 + 255) // 256) * 256
GQ_HEADS = 16
GQ_KV_HEADS = 4
GQ_HEAD_DIM = D_MODEL // GQ_HEADS
GQ_GROUP = GQ_HEADS // GQ_KV_HEADS
GQ_Q_WIDTH = GQ_HEADS * GQ_HEAD_DIM
GQ_KV_WIDTH = GQ_KV_HEADS * GQ_HEAD_DIM
GQ_SCALE = GQ_HEAD_DIM ** -0.5
CV_WIDTH = 31
CV_PAD = CV_WIDTH // 2
DF_HEAD_DIM = 64
DF_HEADS = D_MODEL // (2 * DF_HEAD_DIM)
DF_SCALE = DF_HEAD_DIM ** -0.5
DF_LAMBDA_INIT = 0.470713018
DF_SUBLN_EPS = 1e-5
RW_HEAD_DIM = 64
RW_HEADS = D_MODEL // RW_HEAD_DIM
RW_DECAY_LORA = 64
RW_A_LORA = 64
RW_GATE_LORA = 128
RW_GN_EPS = 64e-5
N_DIRECTIONS = 2

kernel_name = 'hybrid_diffusion_prefix_trunk_step'


def _rmsnorm(x, g, eps=NORM_EPS):
    xf = x.astype(jnp.float32)
    y = xf * lax.rsqrt(jnp.mean(xf * xf, axis=-1, keepdims=True) + eps)
    return (y * g.astype(jnp.float32)).astype(x.dtype)


def _layernorm(x, g, b, eps=LN_EPS):
    xf = x.astype(jnp.float32)
    mu = jnp.mean(xf, axis=-1, keepdims=True)
    var = jnp.mean(jnp.square(xf - mu), axis=-1, keepdims=True)
    return ((xf - mu) * lax.rsqrt(var + eps) * g + b).astype(x.dtype)


def _modulate(h, shift, scale):
    return h * (1.0 + scale) + shift


def _swiglu(h, w_in, w_down):
    gate, up = jnp.split(h @ w_in, 2, axis=-1)
    return (jax.nn.silu(gate) * up) @ w_down


def _axial_rope(n_tokens, head_dim):
    rows = n_tokens // GRID_W
    t = jnp.arange(rows * GRID_W)
    row = (t // GRID_W).astype(jnp.float32)
    col = (t % GRID_W).astype(jnp.float32)
    axis_dim = head_dim // 2
    freqs = ROPE_THETA ** (-jnp.arange(0, axis_dim, 2, dtype=jnp.float32) / axis_dim)
    ang = jnp.concatenate([row[:, None] * freqs, col[:, None] * freqs], axis=-1)
    return jnp.cos(ang), jnp.sin(ang)


def _apply_rope(x, cos, sin):
    shp = x.shape
    xf = x.astype(jnp.float32).reshape(shp[:-1] + (shp[-1] // 2, 2))
    bshape = (1, shp[1]) + (1,) * (x.ndim - 3) + (shp[-1] // 2,)
    c, s = cos.reshape(bshape), sin.reshape(bshape)
    x0, x1 = xf[..., 0], xf[..., 1]
    return jnp.stack([x0 * c - x1 * s, x0 * s + x1 * c], axis=-1).reshape(shp).astype(x.dtype)


def _sweep_query_blocks(block_fn, q):
    b, sq = q.shape[0], q.shape[1]
    nb = sq // Q_BLOCK
    qb = jnp.moveaxis(q.reshape((b, nb, Q_BLOCK) + q.shape[2:]), 1, 0)
    ob = lax.map(block_fn, qb)
    return jnp.moveaxis(ob, 0, 1).reshape((b, sq) + ob.shape[3:])


def _gqa_project(h, gq):
    gq_w_qkv, gq_q_norm, gq_k_norm, _ = gq
    b, s, _ = h.shape
    q, k, v = jnp.split(h @ gq_w_qkv, [GQ_Q_WIDTH, GQ_Q_WIDTH + GQ_KV_WIDTH], axis=-1)
    q = _rmsnorm(q.reshape(b, s, GQ_KV_HEADS, GQ_GROUP, GQ_HEAD_DIM), gq_q_norm)
    k = _rmsnorm(k.reshape(b, s, GQ_KV_HEADS, GQ_HEAD_DIM), gq_k_norm)
    v = v.reshape(b, s, GQ_KV_HEADS, GQ_HEAD_DIM)
    return q, k, v


def _gqa_attend(q, k, v):
    kf, vf = k.astype(jnp.float32), v.astype(jnp.float32)

    def block(qb):
        s = jnp.einsum('bqkgd,bskd->bkgqs', qb.astype(jnp.float32), kf) * GQ_SCALE
        p = jax.nn.softmax(s, axis=-1)
        return jnp.einsum('bkgqs,bskd->bqkgd', p, vf)

    return _sweep_query_blocks(block, q).astype(q.dtype)


def _gqa_context(h, gq):
    q, k, v = _gqa_project(h, gq)
    o = _gqa_attend(q, k, v)
    return o.reshape(h.shape) @ gq[3], (k, v)


def _gqa_latent(h, cache_k, cache_v, gq):
    q, k, v = _gqa_project(h, gq)
    cos, sin = _axial_rope(h.shape[1], GQ_HEAD_DIM)
    q, k = _apply_rope(q, cos, sin), _apply_rope(k, cos, sin)
    k_all = jnp.concatenate([cache_k.astype(k.dtype), k], axis=1)
    v_all = jnp.concatenate([cache_v.astype(v.dtype), v], axis=1)
    o = _gqa_attend(q, k_all, v_all)
    return o.reshape(h.shape) @ gq[3], ()


def _conv_module(h, cv):
    cv_w_in, cv_b_in, cv_w_dw, cv_b_dw, cv_ln_g, cv_ln_b, cv_w_out, cv_b_out = cv
    u_a, u_g = jnp.split(h @ cv_w_in + cv_b_in, 2, axis=-1)
    u = u_a * jax.nn.sigmoid(u_g)
    u = lax.conv_general_dilated(
        u, cv_w_dw[:, None, :].astype(u.dtype), window_strides=(1,),
        padding=[(CV_PAD, CV_PAD)], dimension_numbers=('NWC', 'WIO', 'NWC'),
        feature_group_count=D_MODEL) + cv_b_dw
    u = jax.nn.silu(_layernorm(u, cv_ln_g, cv_ln_b))
    return u @ cv_w_out + cv_b_out, ()


def _diff_lambda(lq1, lk1, lq2, lk2):
    f32 = jnp.float32
    return (jnp.exp(jnp.sum(lq1.astype(f32) * lk1.astype(f32)))
            - jnp.exp(jnp.sum(lq2.astype(f32) * lk2.astype(f32))) + DF_LAMBDA_INIT)


def _diff_project(h, df_w_qkv):
    b, s, _ = h.shape
    q, k, v = jnp.split(h @ df_w_qkv, 3, axis=-1)
    q = q.reshape(b, s, DF_HEADS, 2, DF_HEAD_DIM)
    k = k.reshape(b, s, DF_HEADS, 2, DF_HEAD_DIM)
    v = v.reshape(b, s, DF_HEADS, 2 * DF_HEAD_DIM)
    return q, k, v


def _diff_attend(q, k, v, lam, df_subln_g):
    kf, vf = k.astype(jnp.float32), v.astype(jnp.float32)

    def block(qb):
        s = jnp.einsum('bqhjd,bshjd->bhjqs', qb.astype(jnp.float32), kf) * DF_SCALE
        p = jax.nn.softmax(s, axis=-1)
        return jnp.einsum('bhqs,bshe->bqhe', p[:, :, 0] - lam * p[:, :, 1], vf)

    o = _sweep_query_blocks(block, q)
    o = _rmsnorm(o, df_subln_g, DF_SUBLN_EPS) * (1.0 - DF_LAMBDA_INIT)
    return o.astype(q.dtype)


def _diff_context(h, df_w_qkv, lam, df_subln_g, df_w_o):
    q, k, v = _diff_project(h, df_w_qkv)
    o = _diff_attend(q, k, v, lam, df_subln_g)
    return o.reshape(h.shape) @ df_w_o, (k, v)


def _diff_latent(h, cache_k, cache_v, df_w_qkv, lam, df_subln_g, df_w_o):
    q, k, v = _diff_project(h, df_w_qkv)
    cos, sin = _axial_rope(h.shape[1], DF_HEAD_DIM)
    q, k = _apply_rope(q, cos, sin), _apply_rope(k, cos, sin)
    k_all = jnp.concatenate([cache_k.astype(k.dtype), k], axis=1)
    v_all = jnp.concatenate([cache_v.astype(v.dtype), v], axis=1)
    o = _diff_attend(q, k_all, v_all, lam, df_subln_g)
    return o.reshape(h.shape) @ df_w_o, ()


def _wkv_scan(s0, r, decay, k, v, a, b, reverse):
    def step(s, inp):
        r_t, w_t, k_t, v_t, a_t, b_t = inp
        sa = jnp.einsum('bhvk,bhk->bhv', s, a_t)
        s = s * w_t[:, :, None, :] + sa[..., None] * b_t[:, :, None, :] + v_t[..., None] * k_t[:, :, None, :]
        return s, jnp.einsum('bhvk,bhk->bhv', s, r_t)

    xs = tuple(jnp.swapaxes(t, 0, 1) for t in (r, decay, k, v, a, b))
    s_final, ys = lax.scan(step, s0, xs, reverse=reverse)
    return s_final, jnp.swapaxes(ys, 0, 1)


def _rwkv_mixer(h, s_init, rw):
    (rw_mix, rw_w_r, rw_w_k, rw_w_v, rw_w_o, rw_k_k, rw_k_a, rw_r_k, rw_g1, rw_g2,
     rw_ln_g, rw_ln_b, rw_w0, rw_w1, rw_w2, rw_a0, rw_a1, rw_a2) = rw
    f32 = jnp.float32
    b, t, _ = h.shape

    def heads(z):
        return z.astype(f32).reshape(b, t, RW_HEADS, RW_HEAD_DIM)

    zero = jnp.zeros_like(h[:, :1])
    xx = 0.5 * (jnp.concatenate([zero, h[:, :-1]], axis=1) + jnp.concatenate([h[:, 1:], zero], axis=1)) - h
    xr, xw, xk, xv, xa, xg = (h + xx * rw_mix[i] for i in range(6))
    r = heads(xr @ rw_w_r)
    k = heads(xk @ rw_w_k)
    v = heads(xv @ rw_w_v)
    g = jax.nn.sigmoid(xg @ rw_g1) @ rw_g2
    kk = k * rw_k_k.astype(f32).reshape(RW_HEADS, RW_HEAD_DIM)
    kk = kk * lax.rsqrt(jnp.maximum(jnp.sum(kk * kk, axis=-1, keepdims=True), 1e-24))
    k_a = rw_k_a.astype(f32).reshape(RW_HEADS, RW_HEAD_DIM)
    r_k = rw_r_k.astype(f32)
    ys, bonuses, finals = [], [], []
    for d in range(N_DIRECTIONS):
        w_log = -jax.nn.softplus(-(rw_w0[d] + jnp.tanh(xw @ rw_w1[d]) @ rw_w2[d])) - 0.5
        decay = jnp.exp(-jnp.exp(heads(w_log)))
        a = heads(jax.nn.sigmoid(rw_a0[d] + (xa @ rw_a1[d]) @ rw_a2[d]))
        k_d = k * (1.0 + (a - 1.0) * k_a)
        s_fin, y_d = _wkv_scan(s_init[:, d], r, decay, k_d, v, -kk, kk * a, reverse=(d == 1))
        ys.append(y_d)
        bonuses.append(jnp.sum(r * k_d * r_k, axis=-1, keepdims=True) * v)
        finals.append(s_fin)
    y = ys[0] + ys[1]
    mu = jnp.mean(y, axis=-1, keepdims=True)
    var = jnp.mean(jnp.square(y - mu), axis=-1, keepdims=True)
    yn = ((y - mu) * lax.rsqrt(var + RW_GN_EPS)).reshape(b, t, D_MODEL)
    out = yn * rw_ln_g + rw_ln_b + (bonuses[0] + bonuses[1]).reshape(b, t, D_MODEL)
    return (out.astype(h.dtype) * g) @ rw_w_o, jnp.stack(finals, axis=1)


def _rwkv_context(h, rw):
    s0 = jnp.zeros((h.shape[0], N_DIRECTIONS, RW_HEADS, RW_HEAD_DIM, RW_HEAD_DIM), jnp.float32)
    out, s_final = _rwkv_mixer(h, s0, rw)
    return out, (s_final.astype(h.dtype),)


def _rwkv_latent(h, state, rw):
    out, _ = _rwkv_mixer(h, state.astype(jnp.float32), rw)
    return out, ()


def _trunk_layer(x, cond, mixer, norm_g, mod_w, mod_b, ffn_w_in, ffn_w_down):
    mods = jnp.split((jax.nn.silu(cond) @ mod_w + mod_b)[:, None, :], N_MOD, axis=-1)
    sh1, sc1, gt1, sh2, sc2, gt2, sh3, sc3, gt3 = mods
    x = x + MACARON_WEIGHT * gt1 * _swiglu(_modulate(_rmsnorm(x, norm_g[0]), sh1, sc1), ffn_w_in[0], ffn_w_down[0])
    out, ctx_tensors = mixer(_modulate(_rmsnorm(x, norm_g[1]), sh2, sc2))
    x = x + gt2 * out
    x = x + MACARON_WEIGHT * gt3 * _swiglu(_modulate(_rmsnorm(x, norm_g[2]), sh3, sc3), ffn_w_in[1], ffn_w_down[1])
    return x, ctx_tensors


def setup_inputs(seed: int = 0) -> dict:
    key = jax.random.key(seed)
    keys = iter(jax.random.split(key, 64))
    D = D_MODEL

    def nrm(shape, scale=1.0):
        return scale * jax.random.normal(next(keys), shape, jnp.float32)

    def gain(shape):
        return 1.0 + nrm(shape, 0.05)

    def unif(shape, lo, hi):
        return jax.random.uniform(next(keys), shape, jnp.float32, lo, hi)

    return {
        'x_prompt': nrm((BATCH, SEQ, D)),
        'x_sample': nrm((DEC_BATCH, DEC_SEQ, D)),
        'c': nrm((DEC_BATCH, D)),
        'c_ctx': nrm((D,)),
        'cache_k0': nrm((DEC_BATCH, PAST_LEN, GQ_KV_HEADS, GQ_HEAD_DIM)),
        'cache_v0': nrm((DEC_BATCH, PAST_LEN, GQ_KV_HEADS, GQ_HEAD_DIM)),
        'cache_k2': nrm((DEC_BATCH, PAST_LEN, DF_HEADS, 2, DF_HEAD_DIM)),
        'cache_v2': nrm((DEC_BATCH, PAST_LEN, DF_HEADS, 2 * DF_HEAD_DIM)),
        'state_wkv3': nrm((DEC_BATCH, N_DIRECTIONS, RW_HEADS, RW_HEAD_DIM, RW_HEAD_DIM)),
        'norm_g': gain((DEPTH, 3, D)),
        'mod_w': nrm((DEPTH, D, N_MOD * D), 0.5 * D ** -0.5),
        'mod_b': nrm((DEPTH, N_MOD * D), 0.02),
        'ffn_w_in': nrm((DEPTH, 2, D, 2 * D_FF), D ** -0.5),
        'ffn_w_down': nrm((DEPTH, 2, D_FF, D), D_FF ** -0.5),
        'final_norm_g': gain((D,)),
        'gq_w_qkv': nrm((D, GQ_Q_WIDTH + 2 * GQ_KV_WIDTH), D ** -0.5),
        'gq_q_norm': gain((GQ_HEAD_DIM,)),
        'gq_k_norm': gain((GQ_HEAD_DIM,)),
        'gq_w_o': nrm((D, D), D ** -0.5),
        'cv_w_in': nrm((D, 2 * D), D ** -0.5),
        'cv_b_in': nrm((2 * D,), 0.02),
        'cv_w_dw': nrm((CV_WIDTH, D), CV_WIDTH ** -0.5),
        'cv_b_dw': nrm((D,), 0.02),
        'cv_ln_g': gain((D,)),
        'cv_ln_b': nrm((D,), 0.02),
        'cv_w_out': nrm((D, D), D ** -0.5),
        'cv_b_out': nrm((D,), 0.02),
        'df_w_qkv': nrm((D, 3 * D), D ** -0.5),
        'df_lambda_q1': nrm((DF_HEAD_DIM,), 0.1),
        'df_lambda_k1': nrm((DF_HEAD_DIM,), 0.1),
        'df_lambda_q2': nrm((DF_HEAD_DIM,), 0.1),
        'df_lambda_k2': nrm((DF_HEAD_DIM,), 0.1),
        'df_subln_g': gain((2 * DF_HEAD_DIM,)),
        'df_w_o': nrm((D, D), D ** -0.5),
        'rw_mix': unif((6, D), 0.0, 1.0),
        'rw_w_r': nrm((D, D), D ** -0.5),
        'rw_w_k': nrm((D, D), D ** -0.5),
        'rw_w_v': nrm((D, D), D ** -0.5),
        'rw_w_o': nrm((D, D), D ** -0.5),
        'rw_k_k': 0.85 + nrm((D,), 0.05),
        'rw_k_a': gain((D,)),
        'rw_r_k': nrm((RW_HEADS, RW_HEAD_DIM), 0.1),
        'rw_g1': nrm((D, RW_GATE_LORA), D ** -0.5),
        'rw_g2': nrm((RW_GATE_LORA, D), RW_GATE_LORA ** -0.5),
        'rw_ln_g': gain((D,)),
        'rw_ln_b': nrm((D,), 0.02),
        'rw_w0': unif((N_DIRECTIONS, D), -6.0, 1.0),
        'rw_w1': nrm((N_DIRECTIONS, D, RW_DECAY_LORA), D ** -0.5),
        'rw_w2': nrm((N_DIRECTIONS, RW_DECAY_LORA, D), 0.1 * RW_DECAY_LORA ** -0.5),
        'rw_a0': nrm((N_DIRECTIONS, D), 0.1),
        'rw_a1': nrm((N_DIRECTIONS, D, RW_A_LORA), D ** -0.5),
        'rw_a2': nrm((N_DIRECTIONS, RW_A_LORA, D), 0.1 * RW_A_LORA ** -0.5),
    }


def reference(x_prompt, x_sample, c, c_ctx, cache_k0, cache_v0, cache_k2, cache_v2, state_wkv3,
              norm_g, mod_w, mod_b, ffn_w_in, ffn_w_down, final_norm_g,
              gq_w_qkv, gq_q_norm, gq_k_norm, gq_w_o,
              cv_w_in, cv_b_in, cv_w_dw, cv_b_dw, cv_ln_g, cv_ln_b, cv_w_out, cv_b_out,
              df_w_qkv, df_lambda_q1, df_lambda_k1, df_lambda_q2, df_lambda_k2, df_subln_g, df_w_o,
              rw_mix, rw_w_r, rw_w_k, rw_w_v, rw_w_o, rw_k_k, rw_k_a, rw_r_k, rw_g1, rw_g2,
              rw_ln_g, rw_ln_b, rw_w0, rw_w1, rw_w2, rw_a0, rw_a1, rw_a2):
    gq = (gq_w_qkv, gq_q_norm, gq_k_norm, gq_w_o)
    cv = (cv_w_in, cv_b_in, cv_w_dw, cv_b_dw, cv_ln_g, cv_ln_b, cv_w_out, cv_b_out)
    lam = _diff_lambda(df_lambda_q1, df_lambda_k1, df_lambda_q2, df_lambda_k2)
    rw = (rw_mix, rw_w_r, rw_w_k, rw_w_v, rw_w_o, rw_k_k, rw_k_a, rw_r_k, rw_g1, rw_g2,
          rw_ln_g, rw_ln_b, rw_w0, rw_w1, rw_w2, rw_a0, rw_a1, rw_a2)

    ctx_mixers = (
        lambda h: _gqa_context(h, gq),
        lambda h: _conv_module(h, cv),
        lambda h: _diff_context(h, df_w_qkv, lam, df_subln_g, df_w_o),
        lambda h: _rwkv_context(h, rw),
    )
    lat_mixers = (
        lambda h: _gqa_latent(h, cache_k0, cache_v0, gq),
        lambda h: _conv_module(h, cv),
        lambda h: _diff_latent(h, cache_k2, cache_v2, df_w_qkv, lam, df_subln_g, df_w_o),
        lambda h: _rwkv_latent(h, state_wkv3, rw),
    )

    y = x_prompt
    ctx_tensors = []
    for layer in range(DEPTH):
        y, aux = _trunk_layer(y, c_ctx[None, :], ctx_mixers[layer % N_MIXERS], norm_g[layer],
                              mod_w[layer], mod_b[layer], ffn_w_in[layer], ffn_w_down[layer])
        ctx_tensors.append(aux)
    y_prompt = _rmsnorm(y, final_norm_g)
    (new_k0, new_v0), _, (new_k2, new_v2), (new_wkv3,) = ctx_tensors

    z = x_sample
    for layer in range(DEPTH):
        z, _ = _trunk_layer(z, c, lat_mixers[layer % N_MIXERS], norm_g[layer],
                            mod_w[layer], mod_b[layer], ffn_w_in[layer], ffn_w_down[layer])
    y_sample = _rmsnorm(z, final_norm_g)

    return (y_prompt, y_sample, new_k0, new_v0, new_k2, new_v2, new_wkv3)
```

```python
import functools

import jax
import jax.numpy as jnp
from jax import lax
from jax.experimental import pallas as pl
from jax.experimental.pallas import tpu as pltpu

F32 = jnp.float32
BF16 = jnp.bfloat16

D = 1024
BATCH, SEQ = 32, 256
DEC_BATCH, DEC_SEQ = 2, 2048
PAST = 512
DEPTH = 4
GRID_W = 64
N_MOD = 9
NORM_EPS = 1e-6
LN_EPS = 1e-5
ROPE_THETA = 10000.0
D_FF = 2816
HD = 64
GQ_HEADS, GQ_KV = 16, 4
GQ_GROUP = GQ_HEADS // GQ_KV
GQ_KVW = GQ_KV * HD
CV_WIDTH = 31
CV_PAD = CV_WIDTH // 2
DF_HEADS = 8
DF_LAMBDA_INIT = 0.470713018
DF_SUBLN_EPS = 1e-5
RW_HEADS = 16
RW_GN_EPS = 64e-5
LORA = 64
GATE_LORA = 128

N_CTX = BATCH * SEQ
N_LAT = DEC_BATCH * DEC_SEQ
N_TOK = N_CTX + N_LAT

LANES = 128
TM = 256
TM_FFN = 512
FF_CHUNK = 256
WKV_CHUNK = 64
QUAD = 4 * HD
VMEM_LIMIT = 56 * 1024 * 1024


def _mm(a, b):
    return jnp.dot(a.astype(BF16), b.astype(BF16), preferred_element_type=F32)


def _mm_nt(a, b):
    return lax.dot_general(a.astype(BF16), b.astype(BF16), (((1,), (1,)), ((), ())),
                           preferred_element_type=F32)


def _mm_tn(a, b):
    return lax.dot_general(a.astype(BF16), b.astype(BF16), (((0,), (0,)), ((), ())),
                           preferred_element_type=F32)


def _split3(x):
    hi = x.astype(BF16)
    r1 = x - hi.astype(F32)
    mid = r1.astype(BF16)
    lo = (r1 - mid.astype(F32)).astype(BF16)
    return hi, mid, lo


def _mm3(a, b, nt=False):
    ah = a.astype(BF16)
    al = (a - ah.astype(F32)).astype(BF16)
    bh = b.astype(BF16)
    bl = (b - bh.astype(F32)).astype(BF16)
    dims = (((1,), (1 if nt else 0,)), ((), ()))
    dot = functools.partial(lax.dot_general, dimension_numbers=dims, preferred_element_type=F32)
    return dot(ah, bh) + (dot(ah, bl) + dot(al, bh))


def _silu(x):
    return x * jax.nn.sigmoid(x)


def _rms(x, g, eps=NORM_EPS):
    return x * lax.rsqrt(jnp.mean(x * x, axis=-1, keepdims=True) + eps) * g


def _seg_ones():
    r = lax.broadcasted_iota(jnp.int32, (LANES, LANES), 0) // HD
    c = lax.broadcasted_iota(jnp.int32, (LANES, LANES), 1) // HD
    return (r == c).astype(BF16)


def _seg_sum(x, ones):
    outs = []
    for g in range(x.shape[1] // LANES):
        hi, mid, lo = _split3(x[:, g * LANES:(g + 1) * LANES])
        dot = functools.partial(jnp.dot, preferred_element_type=F32)
        outs.append(dot(hi, ones) + (dot(mid, ones) + dot(lo, ones)))
    return outs[0] if len(outs) == 1 else jnp.concatenate(outs, axis=1)


def _rope(x, cos, sin_s):
    lane = lax.broadcasted_iota(jnp.int32, (x.shape[0], LANES), 1)
    even = (lane % 2) == 0
    outs = []
    for g in range(x.shape[1] // LANES):
        xg = x[:, g * LANES:(g + 1) * LANES]
        partner = jnp.where(even, pltpu.roll(xg, LANES - 1, 1), pltpu.roll(xg, 1, 1))
        outs.append(xg * cos + partner * sin_s)
    return outs[0] if len(outs) == 1 else jnp.concatenate(outs, axis=1)


def _cond_row(i, tm):
    nt_ctx = N_CTX // tm
    return jnp.where(i < nt_ctx, 0, 1 + (i - nt_ctx) // (DEC_SEQ // tm))


def _tok(width, tm=TM):
    return pl.BlockSpec((tm, width), lambda i: (i, 0))


def _full(shape):
    nd = len(shape)
    return pl.BlockSpec(tuple(shape), lambda i: (0,) * nd, pipeline_mode=pl.Buffered(1))


def _modspec(tm=TM):
    return pl.BlockSpec((1, N_MOD, D), lambda i: (_cond_row(i, tm), 0, 0))


def _ropespec():
    nt_ctx = N_CTX // TM
    per_seq = DEC_SEQ // TM
    return pl.BlockSpec((TM, LANES), lambda i: (jnp.where(i < nt_ctx, 0, 1 + (i - nt_ctx) % per_seq), 0))


def _params(n_axes=1):
    return pltpu.CompilerParams(dimension_semantics=("arbitrary",) * n_axes, vmem_limit_bytes=VMEM_LIMIT)


def _row(v):
    return v.reshape(1, -1).astype(F32)


def _mods_kernel(c_ref, w_ref, b_ref, o_ref):
    o_ref[0] = _mm(_silu(c_ref[...]), w_ref[0]) + b_ref[0]


def _mods_call(cond8, mod_w, mod_b):
    out = pl.pallas_call(
        _mods_kernel,
        out_shape=jax.ShapeDtypeStruct((DEPTH, 8, N_MOD * D), F32),
        grid=(DEPTH, N_MOD),
        in_specs=[pl.BlockSpec((8, D), lambda l, j: (0, 0)),
                  pl.BlockSpec((1, D, D), lambda l, j: (l, 0, j)),
                  pl.BlockSpec((1, 1, D), lambda l, j: (l, 0, j))],
        out_specs=pl.BlockSpec((1, 8, D), lambda l, j: (l, 0, j)),
        compiler_params=_params(2),
        name="adaln_mods",
    )(cond8, mod_w, mod_b.reshape(DEPTH, 1, N_MOD * D))
    return out.reshape(DEPTH, 8, N_MOD, D)


def _ffn_kernel(x_ref, mod_ref, g_ref, wg_ref, wu_ref, wd_ref, *rest, row0, final):
    if final:
        fg_ref, o_ref, acc_ref = rest
    else:
        o_ref, acc_ref = rest
    x = x_ref[...]
    m = mod_ref[0]
    h = _rms(x, g_ref[...]) * (1.0 + m[row0 + 1:row0 + 2]) + m[row0:row0 + 1]
    hb = h.astype(BF16)
    acc_ref[...] = jnp.zeros_like(acc_ref)

    def body(c, carry):
        gate = jnp.dot(hb, wg_ref[c], preferred_element_type=F32)
        up = jnp.dot(hb, wu_ref[c], preferred_element_type=F32)
        acc_ref[...] += _mm(_silu(gate) * up, wd_ref[c])
        return carry

    lax.fori_loop(0, D_FF // FF_CHUNK, body, 0)
    y = x + (0.5 * m[row0 + 2:row0 + 3]) * acc_ref[...]
    if final:
        y = _rms(y, fg_ref[...])
    o_ref[...] = y


def _ffn_call(x, mods_l, g, wg, wu, wd, row0, final_g=None):
    final = final_g is not None
    nck = D_FF // FF_CHUNK
    in_specs = [_tok(D, TM_FFN), _modspec(TM_FFN), _full((1, D)),
                _full((nck, D, FF_CHUNK)), _full((nck, D, FF_CHUNK)), _full((nck, FF_CHUNK, D))]
    args = [x, mods_l, _row(g), wg, wu, wd]
    if final:
        in_specs.append(_full((1, D)))
        args.append(_row(final_g))
    return pl.pallas_call(
        functools.partial(_ffn_kernel, row0=row0, final=final),
        out_shape=jax.ShapeDtypeStruct((N_TOK, D), F32),
        grid=(N_TOK // TM_FFN,),
        in_specs=in_specs,
        out_specs=_tok(D, TM_FFN),
        scratch_shapes=[pltpu.VMEM((TM_FFN, D), F32)],
        compiler_params=_params(),
        name="ffn_final" if final else "ffn",
    )(*args)


def _out_kernel(o_ref, x_ref, mod_ref, w_ref, *rest, bias):
    if bias:
        b_ref, y_ref = rest
    else:
        (y_ref,) = rest
    out = _mm(o_ref[...], w_ref[...])
    if bias:
        out = out + b_ref[...]
    y_ref[...] = x_ref[...] + mod_ref[0][5:6] * out


def _out_call(o, x, mods_l, w, b=None):
    bias = b is not None
    in_specs = [_tok(o.shape[1]), _tok(D), _modspec(), _full(w.shape)]
    args = [o, x, mods_l, w]
    if bias:
        in_specs.append(_full((1, D)))
        args.append(_row(b))
    return pl.pallas_call(
        functools.partial(_out_kernel, bias=bias),
        out_shape=jax.ShapeDtypeStruct((N_TOK, D), F32),
        grid=(N_TOK // TM,),
        in_specs=in_specs,
        out_specs=_tok(D),
        compiler_params=_params(),
        name="out_proj",
    )(*args)


def _gqa_proj_kernel(x_ref, mod_ref, g_ref, w_ref, qn_ref, kn_ref, cos_ref, sin_ref, q_ref, k_ref, v_ref):
    m = mod_ref[0]
    h = _rms(x_ref[...], g_ref[...]) * (1.0 + m[4:5]) + m[3:4]
    qkv = _mm(h, w_ref[...])
    ones = _seg_ones()
    cos, sin_s = cos_ref[...], sin_ref[...]
    q = qkv[:, :D]
    k = qkv[:, D:D + GQ_KVW]
    q = q * lax.rsqrt(_seg_sum(q * q, ones) * (1.0 / HD) + NORM_EPS) * qn_ref[...]
    k = k * lax.rsqrt(_seg_sum(k * k, ones) * (1.0 / HD) + NORM_EPS) * kn_ref[...]
    q_ref[...] = _rope(q, cos, sin_s)
    k_ref[...] = _rope(k, cos, sin_s)
    v_ref[...] = qkv[:, D + GQ_KVW:]


def _gqa_proj_call(x, mods_l, g, w, qn, kn, cos_t, sin_t):
    return pl.pallas_call(
        _gqa_proj_kernel,
        out_shape=(jax.ShapeDtypeStruct((N_TOK, D), F32),
                   jax.ShapeDtypeStruct((N_TOK, GQ_KVW), F32),
                   jax.ShapeDtypeStruct((N_TOK, GQ_KVW), F32)),
        grid=(N_TOK // TM,),
        in_specs=[_tok(D), _modspec(), _full((1, D)), _full(w.shape), _full((1, D)), _full((1, GQ_KVW)),
                  _ropespec(), _ropespec()],
        out_specs=(_tok(D), _tok(GQ_KVW), _tok(GQ_KVW)),
        compiler_params=_params(),
        name="gqa_proj",
    )(x, mods_l, _row(g), w, _row(jnp.tile(qn, GQ_HEADS)), _row(jnp.tile(kn, GQ_KV)), cos_t, sin_t)


def _softmax_parts(q, k):
    s = _mm_nt(q, k)
    e = jnp.exp(s - jnp.max(s, axis=-1, keepdims=True))
    return e, jnp.sum(e, axis=-1, keepdims=True)


def _gqa_attn_kernel(q_ref, k_ref, v_ref, o_ref):
    scale = HD ** -0.5
    for kv in range(GQ_KV):
        kh = k_ref[0, :, kv * HD:(kv + 1) * HD]
        vh = v_ref[0, :, kv * HD:(kv + 1) * HD]
        for g in range(GQ_GROUP):
            h = kv * GQ_GROUP + g
            e, l = _softmax_parts(q_ref[0, :, h * HD:(h + 1) * HD] * scale, kh)
            o_ref[0, :, h * HD:(h + 1) * HD] = _mm(e, vh) / l


def _gqa_attn_call(q, k, v, tq):
    b, sq, _ = q.shape
    sk = k.shape[1]
    return pl.pallas_call(
        _gqa_attn_kernel,
        out_shape=jax.ShapeDtypeStruct((b, sq, D), F32),
        grid=(b, sq // tq),
        in_specs=[pl.BlockSpec((1, tq, D), lambda i, j: (i, j, 0)),
                  pl.BlockSpec((1, sk, GQ_KVW), lambda i, j: (i, 0, 0)),
                  pl.BlockSpec((1, sk, GQ_KVW), lambda i, j: (i, 0, 0))],
        out_specs=pl.BlockSpec((1, tq, D), lambda i, j: (i, j, 0)),
        compiler_params=_params(2),
        name="gqa_attn",
    )(q, k, v)


def _conv_in_kernel(x_ref, mod_ref, g_ref, w_ref, b_ref, u_ref):
    m = mod_ref[0]
    h = _rms(x_ref[...], g_ref[...]) * (1.0 + m[4:5]) + m[3:4]
    u = _mm(h, w_ref[...]) + b_ref[...]
    u_ref[...] = u[:, :D] * jax.nn.sigmoid(u[:, D:])


def _conv_in_call(x, mods_l, g, w, b):
    return pl.pallas_call(
        _conv_in_kernel,
        out_shape=jax.ShapeDtypeStruct((N_TOK, D), F32),
        grid=(N_TOK // TM,),
        in_specs=[_tok(D), _modspec(), _full((1, D)), _full(w.shape), _full((1, 2 * D))],
        out_specs=_tok(D),
        compiler_params=_params(),
        name="conv_in",
    )(x, mods_l, _row(g), w, _row(b))


HALO = 16


def _seq_edges(i):
    nt_ctx = N_CTX // TM
    per = jnp.where(i < nt_ctx, SEQ // TM, DEC_SEQ // TM)
    j = jnp.where(i < nt_ctx, i, i - nt_ctx) % per
    return j == 0, j == per - 1


def _conv_out_kernel(u_ref, up_ref, un_ref, x_ref, mod_ref, wdw_ref, bdw_ref, lg_ref, lb_ref, wo_ref, bo_ref,
                     y_ref, pad_ref):
    start, end = _seq_edges(pl.program_id(0))
    pad_ref[0:HALO, :] = jnp.where(start, 0.0, up_ref[...])
    pad_ref[HALO:HALO + TM, :] = u_ref[...]
    pad_ref[HALO + TM:, :] = jnp.where(end, 0.0, un_ref[...])
    wdw = wdw_ref[...]
    acc = jnp.zeros((TM, D), F32)
    for j in range(CV_WIDTH):
        off = HALO - CV_PAD + j
        acc = acc + wdw[j:j + 1] * pad_ref[off:off + TM, :]
    u = acc + bdw_ref[...]
    mu = jnp.mean(u, axis=-1, keepdims=True)
    var = jnp.mean(jnp.square(u - mu), axis=-1, keepdims=True)
    u = _silu((u - mu) * lax.rsqrt(var + LN_EPS) * lg_ref[...] + lb_ref[...])
    out = _mm(u, wo_ref[...]) + bo_ref[...]
    y_ref[...] = x_ref[...] + mod_ref[0][5:6] * out


def _conv_out_call(u, x, mods_l, wdw, bdw, lg, lb, wo, bo):
    per = TM // HALO
    last = N_TOK // HALO - 1
    return pl.pallas_call(
        _conv_out_kernel,
        out_shape=jax.ShapeDtypeStruct((N_TOK, D), F32),
        grid=(N_TOK // TM,),
        in_specs=[_tok(D),
                  pl.BlockSpec((HALO, D), lambda i: (jnp.maximum(i * per - 1, 0), 0)),
                  pl.BlockSpec((HALO, D), lambda i: (jnp.minimum((i + 1) * per, last), 0)),
                  _tok(D), _modspec(), _full((CV_WIDTH, D)), _full((1, D)), _full((1, D)), _full((1, D)),
                  _full((D, D)), _full((1, D))],
        out_specs=_tok(D),
        scratch_shapes=[pltpu.VMEM((TM + 2 * HALO, D), F32)],
        compiler_params=_params(),
        name="conv_out",
    )(u, u, u, x, mods_l, wdw, _row(bdw), _row(lg), _row(lb), wo, _row(bo))


def _diff_proj_kernel(x_ref, mod_ref, g_ref, w_ref, cos_ref, sin_ref, q_ref, k_ref, v_ref):
    m = mod_ref[0]
    h = _rms(x_ref[...], g_ref[...]) * (1.0 + m[4:5]) + m[3:4]
    qkv = _mm(h, w_ref[...])
    cos, sin_s = cos_ref[...], sin_ref[...]
    q_ref[...] = _rope(qkv[:, :D], cos, sin_s)
    k_ref[...] = _rope(qkv[:, D:2 * D], cos, sin_s)
    v_ref[...] = qkv[:, 2 * D:]


def _diff_proj_call(x, mods_l, g, w, cos_t, sin_t):
    return pl.pallas_call(
        _diff_proj_kernel,
        out_shape=tuple(jax.ShapeDtypeStruct((N_TOK, D), F32) for _ in range(3)),
        grid=(N_TOK // TM,),
        in_specs=[_tok(D), _modspec(), _full((1, D)), _full(w.shape), _ropespec(), _ropespec()],
        out_specs=(_tok(D), _tok(D), _tok(D)),
        compiler_params=_params(),
        name="diff_proj",
    )(x, mods_l, _row(g), w, cos_t, sin_t)


def _diff_attn_kernel(q_ref, k_ref, v_ref, lam_ref, sg_ref, o_ref):
    lv = lam_ref[...]
    lam = (jnp.exp(jnp.sum(lv[0:1] * lv[1:2], axis=-1, keepdims=True))
           - jnp.exp(jnp.sum(lv[2:3] * lv[3:4], axis=-1, keepdims=True)) + DF_LAMBDA_INIT)
    scale = HD ** -0.5
    for h in range(DF_HEADS):
        c1, c2 = (2 * h) * HD, (2 * h + 1) * HD
        e1, l1 = _softmax_parts(q_ref[0, :, c1:c1 + HD] * scale, k_ref[0, :, c1:c1 + HD])
        e2, l2 = _softmax_parts(q_ref[0, :, c2:c2 + HD] * scale, k_ref[0, :, c2:c2 + HD])
        p = e1 / l1 - lam * (e2 / l2)
        o = _mm(p, v_ref[0, :, c1:c1 + 2 * HD])
        o = _rms(o, sg_ref[...], DF_SUBLN_EPS) * (1.0 - DF_LAMBDA_INIT)
        o_ref[0, :, c1:c1 + 2 * HD] = o


def _diff_attn_call(q, k, v, lam_rows, subln_g, tq):
    b, sq, _ = q.shape
    sk = k.shape[1]
    return pl.pallas_call(
        _diff_attn_kernel,
        out_shape=jax.ShapeDtypeStruct((b, sq, D), F32),
        grid=(b, sq // tq),
        in_specs=[pl.BlockSpec((1, tq, D), lambda i, j: (i, j, 0)),
                  pl.BlockSpec((1, sk, D), lambda i, j: (i, 0, 0)),
                  pl.BlockSpec((1, sk, D), lambda i, j: (i, 0, 0)),
                  pl.BlockSpec((4, HD), lambda i, j: (0, 0)),
                  pl.BlockSpec((1, 2 * HD), lambda i, j: (0, 0))],
        out_specs=pl.BlockSpec((1, tq, D), lambda i, j: (i, j, 0)),
        compiler_params=_params(2),
        name="diff_attn",
    )(q, k, v, lam_rows, _row(subln_g))


def _rwkv_in_kernel(x_ref, xp_ref, xn_ref, mod_ref, g_ref, mix_ref, wr_ref, wk_ref, wv_ref, g1_ref, g2_ref,
                    w1_ref, w2_ref, w0_ref, a1_ref, a2_ref, a0_ref, kk_ref, ka_ref, rk_ref,
                    r_out, v_out, kn_out, g_out, bonus_out, lw_out, kd_out, al_out):
    m = mod_ref[0]
    gn = g_ref[...]

    def norm_mod(z):
        return _rms(z, gn) * (1.0 + m[4:5]) + m[3:4]

    start, end = _seq_edges(pl.program_id(0))
    h = norm_mod(x_ref[...])
    h_prev = jnp.where(start, 0.0, norm_mod(xp_ref[...])[7:8])
    h_next = jnp.where(end, 0.0, norm_mod(xn_ref[...])[0:1])
    row = lax.broadcasted_iota(jnp.int32, (TM, D), 0)
    h_dn = jnp.where(row == 0, h_prev, pltpu.roll(h, 1, 0))
    h_up = jnp.where(row == TM - 1, h_next, pltpu.roll(h, TM - 1, 0))
    xx = 0.5 * (h_dn + h_up) - h
    mix = mix_ref[...]
    xr, xw, xk, xv, xa, xg = (h + xx * mix[i:i + 1] for i in range(6))

    r = _mm(xr, wr_ref[...])
    k = _mm(xk, wk_ref[...])
    v = _mm(xv, wv_ref[...])
    g = _mm(jax.nn.sigmoid(_mm(xg, g1_ref[...])), g2_ref[...])
    wl = w0_ref[...] + _mm(jnp.tanh(_mm(xw, w1_ref[...])), w2_ref[...])
    softplus = jnp.maximum(-wl, 0.0) + jnp.log(1.0 + jnp.exp(-jnp.abs(wl)))
    log_decay = -jnp.exp(-softplus - 0.5)
    alpha = jax.nn.sigmoid(a0_ref[...] + _mm(_mm(xa, a1_ref[...]), a2_ref[...]))

    ones = _seg_ones()
    kk = k * kk_ref[...]
    kk = kk * lax.rsqrt(jnp.maximum(_seg_sum(kk * kk, ones), 1e-24))
    r_out[...] = r
    v_out[...] = v
    kn_out[...] = kk
    g_out[...] = g
    bonus = jnp.zeros((TM, D), F32)
    for d in range(2):
        al = alpha[:, d * D:(d + 1) * D]
        kd = k * (1.0 + (al - 1.0) * ka_ref[...])
        bonus = bonus + _seg_sum(r * kd * rk_ref[...], ones) * v
        lw_out[d] = log_decay[:, d * D:(d + 1) * D]
        kd_out[d] = kd
        al_out[d] = al
    bonus_out[...] = bonus


def _rwkv_in_call(x, mods_l, g, mix, wr, wk, wv, g1, g2, w1, w2, w0, a1, a2, a0, k_k, k_a, r_k):
    per = TM // 8
    last = N_TOK // 8 - 1
    tok_out = jax.ShapeDtypeStruct((N_TOK, D), F32)
    dir_out = jax.ShapeDtypeStruct((2, N_TOK, D), F32)
    dir_spec = pl.BlockSpec((2, TM, D), lambda i: (0, i, 0))
    return pl.pallas_call(
        _rwkv_in_kernel,
        out_shape=(tok_out,) * 5 + (dir_out,) * 3,
        grid=(N_TOK // TM,),
        in_specs=[_tok(D),
                  pl.BlockSpec((8, D), lambda i: (jnp.maximum(i * per - 1, 0), 0)),
                  pl.BlockSpec((8, D), lambda i: (jnp.minimum((i + 1) * per, last), 0)),
                  _modspec(), _full((1, D)), _full((6, D)),
                  _full((D, D)), _full((D, D)), _full((D, D)), _full((D, GATE_LORA)), _full((GATE_LORA, D)),
                  _full((D, 2 * LORA)), _full((2 * LORA, 2 * D)), _full((1, 2 * D)),
                  _full((D, 2 * LORA)), _full((2 * LORA, 2 * D)), _full((1, 2 * D)),
                  _full((1, D)), _full((1, D)), _full((1, D))],
        out_specs=(_tok(D),) * 5 + (dir_spec,) * 3,
        compiler_params=_params(),
        name="rwkv_in",
    )(x, x, x, mods_l, _row(g), mix, wr, wk, wv, g1, g2, w1, w2, _row(w0), a1, a2, _row(a0),
      _row(k_k), _row(k_a), _row(r_k))


def _block_diag_rows(z):
    lane_head = lax.broadcasted_iota(jnp.int32, z.shape, 1) // HD
    return jnp.concatenate([jnp.where(lane_head == h, z, 0.0) for h in range(QUAD // HD)], axis=0)


def _wkv_kernel(r_ref, v_ref, kn_ref, lw_ref, kd_ref, al_ref, *rest, n_chunks, has_init):
    if has_init:
        s0_ref, y_ref, sfin_ref, s_ref = rest
    else:
        y_ref, sfin_ref, s_ref = rest
    d = pl.program_id(1)
    c = pl.program_id(2)
    C = WKV_CHUNK

    @pl.when(c == 0)
    def _():
        if has_init:
            s_ref[...] = s0_ref[0, 0]
        else:
            s_ref[...] = jnp.zeros_like(s_ref)

    sgn = 1 - 2 * d
    row = lax.broadcasted_iota(jnp.int32, (C, C), 0)
    col = lax.broadcasted_iota(jnp.int32, (C, C), 1)
    incl = (col - row) * sgn <= 0
    row4 = lax.broadcasted_iota(jnp.int32, (C, QUAD), 0)
    col4 = lax.broadcasted_iota(jnp.int32, (C, QUAD), 1) % C
    lag4 = (col4 - row4) * sgn
    strict4 = lag4 < 0
    incl4 = lag4 <= 0
    eye4 = (col4 == row4).astype(F32)
    r2 = lax.broadcasted_iota(jnp.int32, (QUAD, QUAD), 0)
    c2 = lax.broadcasted_iota(jnp.int32, (QUAD, QUAD), 1)
    same_head = (r2 // HD) == (c2 // HD)
    eye_q = r2 == c2

    lw = lw_ref[0]
    hi, mid, lo = _split3(lw)
    inc_b = incl.astype(BF16)
    dot = functools.partial(jnp.dot, preferred_element_type=F32)
    g_incl = dot(inc_b, hi) + (dot(inc_b, mid) + dot(inc_b, lo))
    g_excl = g_incl - lw
    g_tot = jnp.sum(lw, axis=0, keepdims=True)
    kn = kn_ref[...]
    b = kn * al_ref[0]
    kd = kd_ref[0]
    e_neg = jnp.exp(-g_incl)
    e_rem = jnp.exp(g_tot - g_incl)
    a_t = -kn * jnp.exp(g_excl)
    r_t = r_ref[...] * jnp.exp(g_incl)
    b_t = b * e_neg
    k_t = kd * e_neg
    b_h = b * e_rem
    k_h = kd * e_rem
    w_tot = jnp.exp(g_tot)
    v = v_ref[...]

    for q in range(D // QUAD):
        sl = slice(q * QUAD, (q + 1) * QUAD)
        aq, rq, vq = a_t[:, sl], r_t[:, sl], v[:, sl]
        gram = _mm3(jnp.concatenate([aq, rq], axis=0),
                    jnp.concatenate([_block_diag_rows(b_t[:, sl]), _block_diag_rows(k_t[:, sl])], axis=0),
                    nt=True)
        a_ab = jnp.where(strict4, gram[:C, :QUAD], 0.0)
        a_ak = jnp.where(strict4, gram[:C, QUAD:], 0.0)
        a_rb = jnp.where(incl4, gram[C:, :QUAD], 0.0)
        a_rk = jnp.where(incl4, gram[C:, QUAD:], 0.0)
        x = eye4 + a_ab
        p = a_ab
        for _ in range(5):
            p = _mm3(p, _block_diag_rows(p))
            x = x + _mm3(x, _block_diag_rows(p))
        akv = _mm3(a_ak, _block_diag_rows(vq))
        wm = _mm3(x, _block_diag_rows(aq))
        um = _mm3(x, _block_diag_rows(akv))
        qm = rq + _mm(a_rb, _block_diag_rows(wm))
        y0 = _mm(a_rb, _block_diag_rows(um)) + _mm(a_rk, _block_diag_rows(vq))
        g_full = _mm_tn(wm, b_h[:, sl])
        h_full = _mm_tn(um, b_h[:, sl]) + _mm_tn(vq, k_h[:, sl])
        g_bd = jnp.where(same_head, g_full, 0.0) + jnp.where(eye_q, w_tot[:, sl], 0.0)
        h_full = jnp.where(same_head, h_full, 0.0)
        h_lanes = (h_full[0:HD] + h_full[HD:2 * HD]) + (h_full[2 * HD:3 * HD] + h_full[3 * HD:])
        s = s_ref[:, sl]
        y_ref[0, :, sl] = _mm_nt(qm, _block_diag_rows(s)) + y0
        s_ref[:, sl] = _mm(s, g_bd) + h_lanes

    @pl.when(c == n_chunks - 1)
    def _():
        sfin_ref[0, 0] = s_ref[...]


def _wkv_call(r, v, kn, lw, kd, al, n_seq, seq_len, row_off, s0=None):
    C = WKV_CHUNK
    n_chunks = seq_len // C
    off = row_off // C

    def blk(s, d, c):
        return off + s * n_chunks + c + d * (n_chunks - 1 - 2 * c)

    tok = pl.BlockSpec((C, D), lambda s, d, c: (blk(s, d, c), 0))
    dirtok = pl.BlockSpec((1, C, D), lambda s, d, c: (d, blk(s, d, c), 0))
    state = pl.BlockSpec((1, 1, HD, D), lambda s, d, c: (s, d, 0, 0))
    has_init = s0 is not None
    in_specs = [tok, tok, tok, dirtok, dirtok, dirtok] + ([state] if has_init else [])
    args = [r, v, kn, lw, kd, al] + ([s0] if has_init else [])
    ydir = pl.BlockSpec((1, C, D), lambda s, d, c: (d, blk(s, d, c) - off, 0))
    return pl.pallas_call(
        functools.partial(_wkv_kernel, n_chunks=n_chunks, has_init=has_init),
        out_shape=(jax.ShapeDtypeStruct((2, n_seq * seq_len, D), F32),
                   jax.ShapeDtypeStruct((n_seq, 2, HD, D), F32)),
        grid=(n_seq, 2, n_chunks),
        in_specs=in_specs,
        out_specs=(ydir, state),
        scratch_shapes=[pltpu.VMEM((HD, D), F32)],
        compiler_params=_params(3),
        name="wkv_scan",
    )(*args)


def _rwkv_out_kernel(y_ref, bonus_ref, g_ref, x_ref, mod_ref, lg_ref, lb_ref, wo_ref, o_ref):
    ones = _seg_ones()
    y = y_ref[0] + y_ref[1]
    mu = _seg_sum(y, ones) * (1.0 / HD)
    yc = y - mu
    var = _seg_sum(yc * yc, ones) * (1.0 / HD)
    out = yc * lax.rsqrt(var + RW_GN_EPS) * lg_ref[...] + lb_ref[...] + bonus_ref[...]
    o_ref[...] = x_ref[...] + mod_ref[0][5:6] * _mm(out * g_ref[...], wo_ref[...])


def _rwkv_out_call(y, bonus, g, x, mods_l, lg, lb, wo):
    return pl.pallas_call(
        _rwkv_out_kernel,
        out_shape=jax.ShapeDtypeStruct((N_TOK, D), F32),
        grid=(N_TOK // TM,),
        in_specs=[pl.BlockSpec((2, TM, D), lambda i: (0, i, 0)), _tok(D), _tok(D), _tok(D), _modspec(),
                  _full((1, D)), _full((1, D)), _full((D, D))],
        out_specs=_tok(D),
        compiler_params=_params(),
        name="rwkv_out",
    )(y, bonus, g, x, mods_l, _row(lg), _row(lb), wo)


def _rope_tables():
    t = jnp.arange(DEC_SEQ)
    rowp = (t // GRID_W).astype(F32)
    colp = (t % GRID_W).astype(F32)
    axis_dim = HD // 2
    freqs = ROPE_THETA ** (-jnp.arange(0, axis_dim, 2, dtype=F32) / axis_dim)
    ang = jnp.concatenate([rowp[:, None] * freqs, colp[:, None] * freqs], axis=-1)
    cos = jnp.repeat(jnp.cos(ang), 2, axis=-1)
    sin = jnp.repeat(jnp.sin(ang), 2, axis=-1) * jnp.tile(jnp.array([-1.0, 1.0], F32), HD // 2)
    cos = jnp.concatenate([jnp.ones((TM, HD), F32), cos], axis=0)
    sin = jnp.concatenate([jnp.zeros((TM, HD), F32), sin], axis=0)
    return jnp.tile(cos, (1, LANES // HD)), jnp.tile(sin, (1, LANES // HD))


def _block_diag2(w):
    z = jnp.zeros_like(w[0])
    return jnp.concatenate([jnp.concatenate([w[0], z], axis=1), jnp.concatenate([z, w[1]], axis=1)], axis=0)


def kernel(x_prompt, x_sample, c, c_ctx, cache_k0, cache_v0, cache_k2, cache_v2, state_wkv3,
           norm_g, mod_w, mod_b, ffn_w_in, ffn_w_down, final_norm_g,
           gq_w_qkv, gq_q_norm, gq_k_norm, gq_w_o,
           cv_w_in, cv_b_in, cv_w_dw, cv_b_dw, cv_ln_g, cv_ln_b, cv_w_out, cv_b_out,
           df_w_qkv, df_lambda_q1, df_lambda_k1, df_lambda_q2, df_lambda_k2, df_subln_g, df_w_o,
           rw_mix, rw_w_r, rw_w_k, rw_w_v, rw_w_o, rw_k_k, rw_k_a, rw_r_k, rw_g1, rw_g2,
           rw_ln_g, rw_ln_b, rw_w0, rw_w1, rw_w2, rw_a0, rw_a1, rw_a2):
    bf = lambda w: w.astype(BF16)
    nck = D_FF // FF_CHUNK
    x = jnp.concatenate([x_prompt.reshape(N_CTX, D), x_sample.reshape(N_LAT, D)], axis=0)
    cond8 = jnp.concatenate([c_ctx[None, :], c, jnp.zeros((8 - 1 - DEC_BATCH, D), F32)], axis=0)
    mods = _mods_call(cond8, mod_w, mod_b)
    cos_t, sin_t = _rope_tables()

    def ffn(x, layer, which, final_g=None):
        w_in = ffn_w_in[layer, which]
        wg = bf(w_in[:, :D_FF]).reshape(D, nck, FF_CHUNK).transpose(1, 0, 2)
        wu = bf(w_in[:, D_FF:]).reshape(D, nck, FF_CHUNK).transpose(1, 0, 2)
        wd = bf(ffn_w_down[layer, which]).reshape(nck, FF_CHUNK, D)
        return _ffn_call(x, mods[layer], norm_g[layer, 2 * which], wg, wu, wd, 6 * which, final_g)

    def split(z):
        w = z.shape[-1]
        return z[:N_CTX].reshape(BATCH, SEQ, w), z[N_CTX:].reshape(DEC_BATCH, DEC_SEQ, w)

    def join(zc, zl):
        return jnp.concatenate([zc.reshape(N_CTX, -1), zl.reshape(N_LAT, -1)], axis=0)

    x = ffn(x, 0, 0)
    q, k, v = _gqa_proj_call(x, mods[0], norm_g[0, 1], bf(gq_w_qkv), gq_q_norm, gq_k_norm, cos_t, sin_t)
    (qc, ql), (kc, kl), (vc, vl) = split(q), split(k), split(v)
    new_k0 = kc.reshape(BATCH, SEQ, GQ_KV, HD)
    new_v0 = vc.reshape(BATCH, SEQ, GQ_KV, HD)
    oc = _gqa_attn_call(qc, kc, vc, SEQ)
    k_all = jnp.concatenate([cache_k0.reshape(DEC_BATCH, PAST, GQ_KVW), kl], axis=1)
    v_all = jnp.concatenate([cache_v0.reshape(DEC_BATCH, PAST, GQ_KVW), vl], axis=1)
    ol = _gqa_attn_call(ql, k_all, v_all, TM)
    x = _out_call(join(oc, ol), x, mods[0], bf(gq_w_o))
    x = ffn(x, 0, 1)

    x = ffn(x, 1, 0)
    u = _conv_in_call(x, mods[1], norm_g[1, 1], bf(cv_w_in), cv_b_in)
    x = _conv_out_call(u, x, mods[1], cv_w_dw, cv_b_dw, cv_ln_g, cv_ln_b, bf(cv_w_out), cv_b_out)
    x = ffn(x, 1, 1)

    x = ffn(x, 2, 0)
    q, k, v = _diff_proj_call(x, mods[2], norm_g[2, 1], bf(df_w_qkv), cos_t, sin_t)
    (qc, ql), (kc, kl), (vc, vl) = split(q), split(k), split(v)
    new_k2 = kc.reshape(BATCH, SEQ, DF_HEADS, 2, HD)
    new_v2 = vc.reshape(BATCH, SEQ, DF_HEADS, 2 * HD)
    lam_rows = jnp.stack([df_lambda_q1, df_lambda_k1, df_lambda_q2, df_lambda_k2]).astype(F32)
    oc = _diff_attn_call(qc, kc, vc, lam_rows, df_subln_g, SEQ)
    k_all = jnp.concatenate([cache_k2.reshape(DEC_BATCH, PAST, D), kl], axis=1)
    v_all = jnp.concatenate([cache_v2.reshape(DEC_BATCH, PAST, D), vl], axis=1)
    ol = _diff_attn_call(ql, k_all, v_all, lam_rows, df_subln_g, TM)
    x = _out_call(join(oc, ol), x, mods[2], bf(df_w_o))
    x = ffn(x, 2, 1)

    x = ffn(x, 3, 0)
    w1 = bf(jnp.concatenate([rw_w1[0], rw_w1[1]], axis=1))
    a1 = bf(jnp.concatenate([rw_a1[0], rw_a1[1]], axis=1))
    r, v, kn, g, bonus, lw, kd, al = _rwkv_in_call(
        x, mods[3], norm_g[3, 1], rw_mix, bf(rw_w_r), bf(rw_w_k), bf(rw_w_v), bf(rw_g1), bf(rw_g2),
        w1, bf(_block_diag2(rw_w2)), rw_w0.reshape(-1), a1, bf(_block_diag2(rw_a2)), rw_a0.reshape(-1),
        rw_k_k, rw_k_a, rw_r_k.reshape(-1))
    s0 = state_wkv3.transpose(0, 1, 3, 2, 4).reshape(DEC_BATCH, 2, HD, D)
    y_ctx, s_ctx = _wkv_call(r, v, kn, lw, kd, al, BATCH, SEQ, 0)
    y_lat, _ = _wkv_call(r, v, kn, lw, kd, al, DEC_BATCH, DEC_SEQ, N_CTX, s0)
    new_wkv3 = s_ctx.reshape(BATCH, 2, HD, RW_HEADS, HD).transpose(0, 1, 3, 2, 4)
    y = jnp.concatenate([y_ctx, y_lat], axis=1)
    x = _rwkv_out_call(y, bonus, g, x, mods[3], rw_ln_g, rw_ln_b, bf(rw_w_o))
    x = ffn(x, 3, 1, final_norm_g)

    y_prompt = x[:N_CTX].reshape(BATCH, SEQ, D)
    y_sample = x[N_CTX:].reshape(DEC_BATCH, DEC_SEQ, D)
    return (y_prompt, y_sample, new_k0, new_v0, new_k2, new_v2, new_wkv3)
```

```python
import functools

import jax
import jax.numpy as jnp
from jax import lax
from jax.experimental import pallas as pl
from jax.experimental.pallas import tpu as pltpu

F32 = jnp.float32
BF16 = jnp.bfloat16

D = 1024
BATCH, SEQ = 32, 256
DEC_BATCH, DEC_SEQ = 2, 2048
PAST = 512
DEPTH = 4
GRID_W = 64
N_MOD = 9
NORM_EPS = 1e-6
LN_EPS = 1e-5
ROPE_THETA = 10000.0
D_FF = 2816
HD = 64
GQ_HEADS, GQ_KV = 16, 4
GQ_GROUP = GQ_HEADS // GQ_KV
GQ_KVW = GQ_KV * HD
CV_WIDTH = 31
CV_PAD = CV_WIDTH // 2
DF_HEADS = 8
DF_LAMBDA_INIT = 0.470713018
DF_SUBLN_EPS = 1e-5
RW_HEADS = 16
RW_GN_EPS = 64e-5
LORA = 64
GATE_LORA = 128

N_CTX = BATCH * SEQ
N_LAT = DEC_BATCH * DEC_SEQ
N_TOK = N_CTX + N_LAT

LANES = 128
TM = 256
TM_FFN = 512
FF_CHUNK = 256
WKV_CHUNK = 64
QUAD = 4 * HD
VMEM_LIMIT = 56 * 1024 * 1024


def _mm(a, b):
    return jnp.dot(a.astype(BF16), b.astype(BF16), preferred_element_type=F32)


def _mm_nt(a, b):
    return lax.dot_general(a.astype(BF16), b.astype(BF16), (((1,), (1,)), ((), ())),
                           preferred_element_type=F32)


def _mm_tn(a, b):
    return lax.dot_general(a.astype(BF16), b.astype(BF16), (((0,), (0,)), ((), ())),
                           preferred_element_type=F32)


def _split3(x):
    hi = x.astype(BF16)
    r1 = x - hi.astype(F32)
    mid = r1.astype(BF16)
    lo = (r1 - mid.astype(F32)).astype(BF16)
    return hi, mid, lo


def _mm3(a, b, nt=False):
    ah = a.astype(BF16)
    al = (a - ah.astype(F32)).astype(BF16)
    bh = b.astype(BF16)
    bl = (b - bh.astype(F32)).astype(BF16)
    dims = (((1,), (1 if nt else 0,)), ((), ()))
    dot = functools.partial(lax.dot_general, dimension_numbers=dims, preferred_element_type=F32)
    return dot(ah, bh) + (dot(ah, bl) + dot(al, bh))


def _silu(x):
    return x * jax.nn.sigmoid(x)


def _rms(x, g, eps=NORM_EPS):
    return x * lax.rsqrt(jnp.mean(x * x, axis=-1, keepdims=True) + eps) * g


def _seg_ones():
    r = lax.broadcasted_iota(jnp.int32, (LANES, LANES), 0) // HD
    c = lax.broadcasted_iota(jnp.int32, (LANES, LANES), 1) // HD
    return (r == c).astype(BF16)


def _seg_sum(x, ones):
    outs = []
    for g in range(x.shape[1] // LANES):
        hi, mid, lo = _split3(x[:, g * LANES:(g + 1) * LANES])
        dot = functools.partial(jnp.dot, preferred_element_type=F32)
        outs.append(dot(hi, ones) + (dot(mid, ones) + dot(lo, ones)))
    return outs[0] if len(outs) == 1 else jnp.concatenate(outs, axis=1)


def _rope(x, cos, sin_s):
    lane = lax.broadcasted_iota(jnp.int32, (x.shape[0], LANES), 1)
    even = (lane % 2) == 0
    outs = []
    for g in range(x.shape[1] // LANES):
        xg = x[:, g * LANES:(g + 1) * LANES]
        partner = jnp.where(even, pltpu.roll(xg, LANES - 1, 1), pltpu.roll(xg, 1, 1))
        outs.append(xg * cos + partner * sin_s)
    return outs[0] if len(outs) == 1 else jnp.concatenate(outs, axis=1)


def _cond_row(i, tm):
    nt_ctx = N_CTX // tm
    return jnp.where(i < nt_ctx, 0, 1 + (i - nt_ctx) // (DEC_SEQ // tm))


def _tok(width, tm=TM):
    return pl.BlockSpec((tm, width), lambda i: (i, 0))


def _full(shape):
    nd = len(shape)
    return pl.BlockSpec(tuple(shape), lambda i: (0,) * nd, pipeline_mode=pl.Buffered(1))


def _modspec(tm=TM):
    return pl.BlockSpec((1, N_MOD, D), lambda i: (_cond_row(i, tm), 0, 0))


def _ropespec():
    nt_ctx = N_CTX // TM
    per_seq = DEC_SEQ // TM
    return pl.BlockSpec((TM, LANES), lambda i: (jnp.where(i < nt_ctx, 0, 1 + (i - nt_ctx) % per_seq), 0))


def _params(n_axes=1):
    return pltpu.CompilerParams(dimension_semantics=("arbitrary",) * n_axes, vmem_limit_bytes=VMEM_LIMIT)


def _row(v):
    return v.reshape(1, -1).astype(F32)


def _mods_kernel(c_ref, w_ref, b_ref, o_ref):
    o_ref[0] = _mm(_silu(c_ref[...]), w_ref[0]) + b_ref[0]


def _mods_call(cond8, mod_w, mod_b):
    out = pl.pallas_call(
        _mods_kernel,
        out_shape=jax.ShapeDtypeStruct((DEPTH, 8, N_MOD * D), F32),
        grid=(DEPTH, N_MOD),
        in_specs=[pl.BlockSpec((8, D), lambda l, j: (0, 0)),
                  pl.BlockSpec((1, D, D), lambda l, j: (l, 0, j)),
                  pl.BlockSpec((1, 1, D), lambda l, j: (l, 0, j))],
        out_specs=pl.BlockSpec((1, 8, D), lambda l, j: (l, 0, j)),
        compiler_params=_params(2),
        name="adaln_mods",
    )(cond8, mod_w, mod_b.reshape(DEPTH, 1, N_MOD * D))
    return out.reshape(DEPTH, 8, N_MOD, D)


def _ffn_kernel(x_ref, mod_ref, g_ref, win_ref, wd_ref, *rest, row0, final):
    if final:
        fg_ref, o_ref, act_ref = rest
    else:
        o_ref, act_ref = rest
    x = x_ref[...]
    m = mod_ref[0]
    h = _rms(x, g_ref[...]) * (1.0 + m[row0 + 1:row0 + 2]) + m[row0:row0 + 1]
    hb = h.astype(BF16)
    for c in range(D_FF // FF_CHUNK):
        lo = c * FF_CHUNK
        gate = jnp.dot(hb, win_ref[:, lo:lo + FF_CHUNK], preferred_element_type=F32)
        up = jnp.dot(hb, win_ref[:, D_FF + lo:D_FF + lo + FF_CHUNK], preferred_element_type=F32)
        act_ref[:, lo:lo + FF_CHUNK] = (_silu(gate) * up).astype(BF16)
    out = jnp.dot(act_ref[...], wd_ref[...], preferred_element_type=F32)
    y = x + (0.5 * m[row0 + 2:row0 + 3]) * out
    if final:
        y = _rms(y, fg_ref[...])
    o_ref[...] = y


def _ffn_call(x, mods_l, g, w_in, w_down, row0, final_g=None):
    final = final_g is not None
    in_specs = [_tok(D, TM_FFN), _modspec(TM_FFN), _full((1, D)), _full((D, 2 * D_FF)), _full((D_FF, D))]
    args = [x, mods_l, _row(g), w_in, w_down]
    if final:
        in_specs.append(_full((1, D)))
        args.append(_row(final_g))
    return pl.pallas_call(
        functools.partial(_ffn_kernel, row0=row0, final=final),
        out_shape=jax.ShapeDtypeStruct((N_TOK, D), F32),
        grid=(N_TOK // TM_FFN,),
        in_specs=in_specs,
        out_specs=_tok(D, TM_FFN),
        scratch_shapes=[pltpu.VMEM((TM_FFN, D_FF), BF16)],
        compiler_params=_params(),
        name="ffn_final" if final else "ffn",
    )(*args)


def _ctx_lat_specs(block, tm=TM):
    nt_ctx = N_CTX // tm
    lead = (0,) * (len(block) - 2)
    return (pl.BlockSpec(block, lambda i: lead + (jnp.minimum(i, nt_ctx - 1), 0)),
            pl.BlockSpec(block, lambda i: lead + (jnp.maximum(i - nt_ctx, 0), 0)))


def _ctx_or_lat(c_ref, l_ref, tm=TM):
    return jnp.where(pl.program_id(0) < N_CTX // tm, c_ref[...], l_ref[...])


def _out_kernel(oc_ref, ol_ref, x_ref, mod_ref, w_ref, y_ref):
    out = _mm(_ctx_or_lat(oc_ref, ol_ref), w_ref[...])
    y_ref[...] = x_ref[...] + mod_ref[0][5:6] * out


def _out_call(oc, ol, x, mods_l, w):
    return pl.pallas_call(
        _out_kernel,
        out_shape=jax.ShapeDtypeStruct((N_TOK, D), F32),
        grid=(N_TOK // TM,),
        in_specs=[*_ctx_lat_specs((TM, oc.shape[1])), _tok(D), _modspec(), _full(w.shape)],
        out_specs=_tok(D),
        compiler_params=_params(),
        name="out_proj",
    )(oc, ol, x, mods_l, w)


def _gqa_proj_kernel(x_ref, mod_ref, g_ref, w_ref, qn_ref, kn_ref, cos_ref, sin_ref, q_ref, k_ref, v_ref):
    m = mod_ref[0]
    h = _rms(x_ref[...], g_ref[...]) * (1.0 + m[4:5]) + m[3:4]
    qkv = _mm(h, w_ref[...])
    ones = _seg_ones()
    cos, sin_s = cos_ref[...], sin_ref[...]
    q = qkv[:, :D]
    k = qkv[:, D:D + GQ_KVW]
    q = q * lax.rsqrt(_seg_sum(q * q, ones) * (1.0 / HD) + NORM_EPS) * qn_ref[...]
    k = k * lax.rsqrt(_seg_sum(k * k, ones) * (1.0 / HD) + NORM_EPS) * kn_ref[...]
    q_ref[...] = _rope(q, cos, sin_s)
    k_ref[...] = _rope(k, cos, sin_s)
    v_ref[...] = qkv[:, D + GQ_KVW:]


def _gqa_proj_call(x, mods_l, g, w, qn, kn, cos_t, sin_t):
    return pl.pallas_call(
        _gqa_proj_kernel,
        out_shape=(jax.ShapeDtypeStruct((N_TOK, D), F32),
                   jax.ShapeDtypeStruct((N_TOK, GQ_KVW), F32),
                   jax.ShapeDtypeStruct((N_TOK, GQ_KVW), F32)),
        grid=(N_TOK // TM,),
        in_specs=[_tok(D), _modspec(), _full((1, D)), _full(w.shape), _full((1, D)), _full((1, GQ_KVW)),
                  _ropespec(), _ropespec()],
        out_specs=(_tok(D), _tok(GQ_KVW), _tok(GQ_KVW)),
        compiler_params=_params(),
        name="gqa_proj",
    )(x, mods_l, _row(g), w, _row(jnp.tile(qn, GQ_HEADS)), _row(jnp.tile(kn, GQ_KV)), cos_t, sin_t)


def _softmax_parts(q, ks):
    ss = [_mm_nt(q, k) for k in ks]
    mx = functools.reduce(jnp.maximum, [jnp.max(s, axis=-1, keepdims=True) for s in ss])
    es = [jnp.exp(s - mx) for s in ss]
    return es, functools.reduce(jnp.add, [jnp.sum(e, axis=-1, keepdims=True) for e in es])


def _attn_specs(latent, kv_width):
    if not latent:
        tok = pl.BlockSpec((SEQ, D), lambda i: (i, 0))
        kv = pl.BlockSpec((SEQ, kv_width), lambda i: (i, 0))
        return (BATCH,), tok, [kv], tok
    per = DEC_SEQ // TM
    q = pl.BlockSpec((TM, D), lambda b, j: (N_CTX // TM + b * per + j, 0))
    cache = pl.BlockSpec((PAST, kv_width), lambda b, j: (b, 0))
    own = pl.BlockSpec((DEC_SEQ, kv_width), lambda b, j: (N_CTX // DEC_SEQ + b, 0))
    out = pl.BlockSpec((TM, D), lambda b, j: (b * per + j, 0))
    return (DEC_BATCH, per), q, [cache, own], out


def _gqa_attn_kernel(q_ref, *refs):
    o_ref = refs[-1]
    k_refs, v_refs = refs[0:-1:2], refs[1:-1:2]
    scale = HD ** -0.5
    for kv in range(GQ_KV):
        cols = slice(kv * HD, (kv + 1) * HD)
        ks = [r[:, cols] for r in k_refs]
        vs = [r[:, cols].astype(BF16) for r in v_refs]
        for g in range(GQ_GROUP):
            h = kv * GQ_GROUP + g
            es, l = _softmax_parts(q_ref[:, h * HD:(h + 1) * HD] * scale, ks)
            pv = functools.reduce(jnp.add, [_mm(e, v) for e, v in zip(es, vs)])
            o_ref[:, h * HD:(h + 1) * HD] = pv / l


def _gqa_attn_call(q, k, v, cache_k=None, cache_v=None):
    latent = cache_k is not None
    grid, q_spec, kv_specs, o_spec = _attn_specs(latent, GQ_KVW)
    parts = [(cache_k, cache_v), (k, v)] if latent else [(k, v)]
    return pl.pallas_call(
        _gqa_attn_kernel,
        out_shape=jax.ShapeDtypeStruct((N_LAT if latent else N_CTX, D), F32),
        grid=grid,
        in_specs=[q_spec] + [s for s in kv_specs for _ in range(2)],
        out_specs=o_spec,
        compiler_params=_params(len(grid)),
        name="gqa_attn",
    )(q, *[a for part in parts for a in part])


def _conv_in_kernel(x_ref, mod_ref, g_ref, w_ref, b_ref, u_ref):
    m = mod_ref[0]
    h = _rms(x_ref[...], g_ref[...]) * (1.0 + m[4:5]) + m[3:4]
    u = _mm(h, w_ref[...]) + b_ref[...]
    u_ref[...] = u[:, :D] * jax.nn.sigmoid(u[:, D:])


def _conv_in_call(x, mods_l, g, w, b):
    return pl.pallas_call(
        _conv_in_kernel,
        out_shape=jax.ShapeDtypeStruct((N_TOK, D), F32),
        grid=(N_TOK // TM,),
        in_specs=[_tok(D), _modspec(), _full((1, D)), _full(w.shape), _full((1, 2 * D))],
        out_specs=_tok(D),
        compiler_params=_params(),
        name="conv_in",
    )(x, mods_l, _row(g), w, _row(b))


HALO = 16


def _seq_edges(i):
    nt_ctx = N_CTX // TM
    per = jnp.where(i < nt_ctx, SEQ // TM, DEC_SEQ // TM)
    j = jnp.where(i < nt_ctx, i, i - nt_ctx) % per
    return j == 0, j == per - 1


def _conv_out_kernel(u_ref, up_ref, un_ref, x_ref, mod_ref, wdw_ref, bdw_ref, lg_ref, lb_ref, wo_ref, bo_ref,
                     y_ref, pad_ref):
    start, end = _seq_edges(pl.program_id(0))
    pad_ref[0:HALO, :] = jnp.where(start, 0.0, up_ref[...])
    pad_ref[HALO:HALO + TM, :] = u_ref[...]
    pad_ref[HALO + TM:, :] = jnp.where(end, 0.0, un_ref[...])
    wdw = wdw_ref[...]
    acc = jnp.zeros((TM, D), F32)
    for j in range(CV_WIDTH):
        off = HALO - CV_PAD + j
        acc = acc + wdw[j:j + 1] * pad_ref[off:off + TM, :]
    u = acc + bdw_ref[...]
    mu = jnp.mean(u, axis=-1, keepdims=True)
    var = jnp.mean(jnp.square(u - mu), axis=-1, keepdims=True)
    u = _silu((u - mu) * lax.rsqrt(var + LN_EPS) * lg_ref[...] + lb_ref[...])
    out = _mm(u, wo_ref[...]) + bo_ref[...]
    y_ref[...] = x_ref[...] + mod_ref[0][5:6] * out


def _conv_out_call(u, x, mods_l, wdw, bdw, lg, lb, wo, bo):
    per = TM // HALO
    last = N_TOK // HALO - 1
    return pl.pallas_call(
        _conv_out_kernel,
        out_shape=jax.ShapeDtypeStruct((N_TOK, D), F32),
        grid=(N_TOK // TM,),
        in_specs=[_tok(D),
                  pl.BlockSpec((HALO, D), lambda i: (jnp.maximum(i * per - 1, 0), 0)),
                  pl.BlockSpec((HALO, D), lambda i: (jnp.minimum((i + 1) * per, last), 0)),
                  _tok(D), _modspec(), _full((CV_WIDTH, D)), _full((1, D)), _full((1, D)), _full((1, D)),
                  _full((D, D)), _full((1, D))],
        out_specs=_tok(D),
        scratch_shapes=[pltpu.VMEM((TM + 2 * HALO, D), F32)],
        compiler_params=_params(),
        name="conv_out",
    )(u, u, u, x, mods_l, wdw, _row(bdw), _row(lg), _row(lb), wo, _row(bo))


def _diff_proj_kernel(x_ref, mod_ref, g_ref, w_ref, cos_ref, sin_ref, q_ref, k_ref, v_ref):
    m = mod_ref[0]
    h = _rms(x_ref[...], g_ref[...]) * (1.0 + m[4:5]) + m[3:4]
    qkv = _mm(h, w_ref[...])
    cos, sin_s = cos_ref[...], sin_ref[...]
    q_ref[...] = _rope(qkv[:, :D], cos, sin_s)
    k_ref[...] = _rope(qkv[:, D:2 * D], cos, sin_s)
    v_ref[...] = qkv[:, 2 * D:]


def _diff_proj_call(x, mods_l, g, w, cos_t, sin_t):
    return pl.pallas_call(
        _diff_proj_kernel,
        out_shape=tuple(jax.ShapeDtypeStruct((N_TOK, D), F32) for _ in range(3)),
        grid=(N_TOK // TM,),
        in_specs=[_tok(D), _modspec(), _full((1, D)), _full(w.shape), _ropespec(), _ropespec()],
        out_specs=(_tok(D), _tok(D), _tok(D)),
        compiler_params=_params(),
        name="diff_proj",
    )(x, mods_l, _row(g), w, cos_t, sin_t)


def _diff_attn_kernel(q_ref, lam_ref, sg_ref, *refs):
    o_ref = refs[-1]
    k_refs, v_refs = refs[0:-1:2], refs[1:-1:2]
    lv = lam_ref[...]
    lam = (jnp.exp(jnp.sum(lv[0:1] * lv[1:2], axis=-1, keepdims=True))
           - jnp.exp(jnp.sum(lv[2:3] * lv[3:4], axis=-1, keepdims=True)) + DF_LAMBDA_INIT)
    scale = HD ** -0.5
    for h in range(DF_HEADS):
        c1, c2 = (2 * h) * HD, (2 * h + 1) * HD
        es1, l1 = _softmax_parts(q_ref[:, c1:c1 + HD] * scale, [r[:, c1:c1 + HD] for r in k_refs])
        es2, l2 = _softmax_parts(q_ref[:, c2:c2 + HD] * scale, [r[:, c2:c2 + HD] for r in k_refs])
        inv1, inv2 = 1.0 / l1, lam / l2
        o = functools.reduce(jnp.add, [_mm(e1 * inv1 - e2 * inv2, r[:, c1:c1 + 2 * HD])
                                       for e1, e2, r in zip(es1, es2, v_refs)])
        o = _rms(o, sg_ref[...], DF_SUBLN_EPS) * (1.0 - DF_LAMBDA_INIT)
        o_ref[:, c1:c1 + 2 * HD] = o


def _diff_attn_call(q, k, v, lam_rows, subln_g, cache_k=None, cache_v=None):
    latent = cache_k is not None
    grid, q_spec, kv_specs, o_spec = _attn_specs(latent, D)
    parts = [(cache_k, cache_v), (k, v)] if latent else [(k, v)]
    nd = len(grid)
    return pl.pallas_call(
        _diff_attn_kernel,
        out_shape=jax.ShapeDtypeStruct((N_LAT if latent else N_CTX, D), F32),
        grid=grid,
        in_specs=[q_spec, pl.BlockSpec((4, HD), lambda *_: (0, 0)), pl.BlockSpec((1, 2 * HD), lambda *_: (0, 0))]
        + [s for s in kv_specs for _ in range(2)],
        out_specs=o_spec,
        compiler_params=_params(nd),
        name="diff_attn",
    )(q, lam_rows, _row(subln_g), *[a for part in parts for a in part])


def _rwkv_in_kernel(x_ref, xp_ref, xn_ref, mod_ref, g_ref, mix_ref, wr_ref, wk_ref, wv_ref, g1_ref, g2_ref,
                    w1_ref, w2_ref, w0_ref, a1_ref, a2_ref, a0_ref, kk_ref, ka_ref, rk_ref,
                    r_out, v_out, kn_out, g_out, bonus_out, lw_out, kd_out, al_out):
    m = mod_ref[0]
    gn = g_ref[...]

    def norm_mod(z):
        return _rms(z, gn) * (1.0 + m[4:5]) + m[3:4]

    start, end = _seq_edges(pl.program_id(0))
    h = norm_mod(x_ref[...])
    h_prev = jnp.where(start, 0.0, norm_mod(xp_ref[...])[7:8])
    h_next = jnp.where(end, 0.0, norm_mod(xn_ref[...])[0:1])
    row = lax.broadcasted_iota(jnp.int32, (TM, D), 0)
    h_dn = jnp.where(row == 0, h_prev, pltpu.roll(h, 1, 0))
    h_up = jnp.where(row == TM - 1, h_next, pltpu.roll(h, TM - 1, 0))
    xx = 0.5 * (h_dn + h_up) - h
    mix = mix_ref[...]
    xr, xw, xk, xv, xa, xg = (h + xx * mix[i:i + 1] for i in range(6))

    r = _mm(xr, wr_ref[...])
    k = _mm(xk, wk_ref[...])
    v = _mm(xv, wv_ref[...])
    g = _mm(jax.nn.sigmoid(_mm(xg, g1_ref[...])), g2_ref[...])
    wl = w0_ref[...] + _mm(jnp.tanh(_mm(xw, w1_ref[...])), w2_ref[...])
    softplus = jnp.maximum(-wl, 0.0) + jnp.log(1.0 + jnp.exp(-jnp.abs(wl)))
    log_decay = -jnp.exp(-softplus - 0.5)
    alpha = jax.nn.sigmoid(a0_ref[...] + _mm(_mm(xa, a1_ref[...]), a2_ref[...]))

    ones = _seg_ones()
    kk = k * kk_ref[...]
    kk = kk * lax.rsqrt(jnp.maximum(_seg_sum(kk * kk, ones), 1e-24))
    r_out[...] = r
    v_out[...] = v
    kn_out[...] = kk
    g_out[...] = g
    bonus = jnp.zeros((TM, D), F32)
    for d in range(2):
        al = alpha[:, d * D:(d + 1) * D]
        kd = k * (1.0 + (al - 1.0) * ka_ref[...])
        bonus = bonus + _seg_sum(r * kd * rk_ref[...], ones) * v
        lw_out[d] = log_decay[:, d * D:(d + 1) * D]
        kd_out[d] = kd
        al_out[d] = al
    bonus_out[...] = bonus


def _rwkv_in_call(x, mods_l, g, mix, wr, wk, wv, g1, g2, w1, w2, w0, a1, a2, a0, k_k, k_a, r_k):
    per = TM // 8
    last = N_TOK // 8 - 1
    tok_out = jax.ShapeDtypeStruct((N_TOK, D), F32)
    dir_out = jax.ShapeDtypeStruct((2, N_TOK, D), F32)
    dir_spec = pl.BlockSpec((2, TM, D), lambda i: (0, i, 0))
    return pl.pallas_call(
        _rwkv_in_kernel,
        out_shape=(tok_out,) * 5 + (dir_out,) * 3,
        grid=(N_TOK // TM,),
        in_specs=[_tok(D),
                  pl.BlockSpec((8, D), lambda i: (jnp.maximum(i * per - 1, 0), 0)),
                  pl.BlockSpec((8, D), lambda i: (jnp.minimum((i + 1) * per, last), 0)),
                  _modspec(), _full((1, D)), _full((6, D)),
                  _full((D, D)), _full((D, D)), _full((D, D)), _full((D, GATE_LORA)), _full((GATE_LORA, D)),
                  _full((D, 2 * LORA)), _full((2 * LORA, 2 * D)), _full((1, 2 * D)),
                  _full((D, 2 * LORA)), _full((2 * LORA, 2 * D)), _full((1, 2 * D)),
                  _full((1, D)), _full((1, D)), _full((1, D))],
        out_specs=(_tok(D),) * 5 + (dir_spec,) * 3,
        compiler_params=_params(),
        name="rwkv_in",
    )(x, x, x, mods_l, _row(g), mix, wr, wk, wv, g1, g2, w1, w2, _row(w0), a1, a2, _row(a0),
      _row(k_k), _row(k_a), _row(r_k))


def _block_diag_rows(z):
    zb = z.astype(BF16)
    lane_head = lax.broadcasted_iota(jnp.int32, zb.shape, 1) // HD
    zero = jnp.zeros_like(zb)
    return jnp.concatenate([jnp.where(lane_head == h, zb, zero) for h in range(QUAD // HD)], axis=0)


def _mm_bd(a, z, hp, nt=False):
    dims = (((1,), (1 if nt else 0,)), ((), ()))
    dot = functools.partial(lax.dot_general, dimension_numbers=dims, preferred_element_type=F32)
    ah = a.astype(BF16)
    zh = z.astype(BF16)
    if not hp:
        return dot(ah, _block_diag_rows(zh))
    al = (a - ah.astype(F32)).astype(BF16)
    zl = (z - zh.astype(F32)).astype(BF16)
    zh_bd = _block_diag_rows(zh)
    return dot(ah, zh_bd) + (dot(ah, _block_diag_rows(zl)) + dot(al, zh_bd))


def _wkv_kernel(r_ref, v_ref, kn_ref, lw_ref, kd_ref, al_ref, *rest, n_chunks, has_init):
    if has_init:
        s0_ref, y_ref, sfin_ref, s_ref = rest
    else:
        y_ref, sfin_ref, s_ref = rest
    d = pl.program_id(1)
    c = pl.program_id(2)
    C = WKV_CHUNK

    @pl.when(c == 0)
    def _():
        if has_init:
            s_ref[...] = s0_ref[0, 0]
        else:
            s_ref[...] = jnp.zeros_like(s_ref)

    sgn = 1 - 2 * d
    row = lax.broadcasted_iota(jnp.int32, (C, C), 0)
    col = lax.broadcasted_iota(jnp.int32, (C, C), 1)
    incl = (col - row) * sgn <= 0
    row4 = lax.broadcasted_iota(jnp.int32, (C, QUAD), 0)
    col4 = lax.broadcasted_iota(jnp.int32, (C, QUAD), 1) % C
    lag4 = (col4 - row4) * sgn
    strict4 = lag4 < 0
    incl4 = lag4 <= 0
    eye4 = (col4 == row4).astype(F32)
    r2 = lax.broadcasted_iota(jnp.int32, (QUAD, QUAD), 0)
    c2 = lax.broadcasted_iota(jnp.int32, (QUAD, QUAD), 1)
    same_head = (r2 // HD) == (c2 // HD)

    lw = lw_ref[0]
    hi, mid, lo = _split3(lw)
    inc_b = incl.astype(BF16)
    dot = functools.partial(jnp.dot, preferred_element_type=F32)
    g_incl = dot(inc_b, hi) + (dot(inc_b, mid) + dot(inc_b, lo))
    g_excl = g_incl - lw
    g_tot = jnp.sum(lw, axis=0, keepdims=True)
    kn = kn_ref[...]
    b = kn * al_ref[0]
    kd = kd_ref[0]
    e_neg = jnp.exp(-g_incl)
    e_rem = jnp.exp(g_tot - g_incl)
    a_t = -kn * jnp.exp(g_excl)
    r_t = r_ref[...] * jnp.exp(g_incl)
    b_t = b * e_neg
    k_t = kd * e_neg
    b_h = b * e_rem
    k_h = kd * e_rem
    w_tot = jnp.exp(g_tot)
    v = v_ref[...]

    quads = [slice(q * QUAD, (q + 1) * QUAD) for q in range(D // QUAD)]
    bd = _block_diag_rows
    s_in = [s_ref[:, sl] for sl in quads]
    ar = [jnp.concatenate([a_t[:, sl], r_t[:, sl]], axis=0) for sl in quads]
    gram_b = [_mm_bd(z, b_t[:, sl], False, nt=True) for z, sl in zip(ar, quads)]
    gram_k = [_mm_bd(z, k_t[:, sl], False, nt=True) for z, sl in zip(ar, quads)]
    a_ab = [jnp.where(strict4, g[:C], 0.0) for g in gram_b]
    a_ak = [jnp.where(strict4, g[:C], 0.0) for g in gram_k]
    a_rb = [jnp.where(incl4, g[C:], 0.0) for g in gram_b]
    a_rk = [jnp.where(incl4, g[C:], 0.0) for g in gram_k]
    x = [eye4 + l for l in a_ab]
    p = a_ab
    for _ in range(5):
        p = [_mm_bd(pq, pq, True) for pq in p]
        x = [xq + _mm_bd(xq, pq, True) for xq, pq in zip(x, p)]
    vbd = [bd(v[:, sl]) for sl in quads]
    akv = [_mm(m, vb) for m, vb in zip(a_ak, vbd)]
    wm = [_mm(xq, bd(a_t[:, sl])) for xq, sl in zip(x, quads)]
    um = [_mm(xq, bd(z)) for xq, z in zip(x, akv)]
    qm = [r_t[:, sl] + _mm(m, bd(w)) for sl, m, w in zip(quads, a_rb, wm)]
    y0 = [_mm(mb, bd(u)) + _mm(mk, vb) for mb, u, mk, vb in zip(a_rb, um, a_rk, vbd)]
    sbd = [bd(sq) for sq in s_in]
    u = [_mm_nt(w, sb) + z for w, sb, z in zip(wm, sbd, um)]
    for sl, qq, sb, yy in zip(quads, qm, sbd, y0):
        y_ref[0, :, sl] = _mm_nt(qq, sb) + yy
    for sl, uq, sq in zip(quads, u, s_in):
        upd = _mm_tn(jnp.concatenate([uq, v[:, sl]], axis=0),
                     jnp.concatenate([b_h[:, sl], k_h[:, sl]], axis=0))
        upd = jnp.where(same_head, upd, 0.0)
        upd = (upd[0:HD] + upd[HD:2 * HD]) + (upd[2 * HD:3 * HD] + upd[3 * HD:])
        s_ref[:, sl] = sq * w_tot[:, sl] + upd

    @pl.when(c == n_chunks - 1)
    def _():
        sfin_ref[0, 0] = s_ref[...]


def _wkv_call(r, v, kn, lw, kd, al, n_seq, seq_len, row_off, s0=None):
    C = WKV_CHUNK
    n_chunks = seq_len // C
    off = row_off // C

    def blk(s, d, c):
        return off + s * n_chunks + c + d * (n_chunks - 1 - 2 * c)

    tok = pl.BlockSpec((C, D), lambda s, d, c: (blk(s, d, c), 0))
    dirtok = pl.BlockSpec((1, C, D), lambda s, d, c: (d, blk(s, d, c), 0))
    state = pl.BlockSpec((1, 1, HD, D), lambda s, d, c: (s, d, 0, 0))
    has_init = s0 is not None
    in_specs = [tok, tok, tok, dirtok, dirtok, dirtok] + ([state] if has_init else [])
    args = [r, v, kn, lw, kd, al] + ([s0] if has_init else [])
    ydir = pl.BlockSpec((1, C, D), lambda s, d, c: (d, blk(s, d, c) - off, 0))
    return pl.pallas_call(
        functools.partial(_wkv_kernel, n_chunks=n_chunks, has_init=has_init),
        out_shape=(jax.ShapeDtypeStruct((2, n_seq * seq_len, D), F32),
                   jax.ShapeDtypeStruct((n_seq, 2, HD, D), F32)),
        grid=(n_seq, 2, n_chunks),
        in_specs=in_specs,
        out_specs=(ydir, state),
        scratch_shapes=[pltpu.VMEM((HD, D), F32)],
        compiler_params=_params(3),
        name="wkv_scan",
    )(*args)


def _rwkv_out_kernel(yc_ref, yl_ref, bonus_ref, g_ref, x_ref, mod_ref, lg_ref, lb_ref, wo_ref, o_ref):
    ones = _seg_ones()
    y2 = _ctx_or_lat(yc_ref, yl_ref)
    y = y2[0] + y2[1]
    mu = _seg_sum(y, ones) * (1.0 / HD)
    yc = y - mu
    var = _seg_sum(yc * yc, ones) * (1.0 / HD)
    out = yc * lax.rsqrt(var + RW_GN_EPS) * lg_ref[...] + lb_ref[...] + bonus_ref[...]
    o_ref[...] = x_ref[...] + mod_ref[0][5:6] * _mm(out * g_ref[...], wo_ref[...])


def _rwkv_out_call(y_ctx, y_lat, bonus, g, x, mods_l, lg, lb, wo):
    return pl.pallas_call(
        _rwkv_out_kernel,
        out_shape=jax.ShapeDtypeStruct((N_TOK, D), F32),
        grid=(N_TOK // TM,),
        in_specs=[*_ctx_lat_specs((2, TM, D)), _tok(D), _tok(D), _tok(D), _modspec(),
                  _full((1, D)), _full((1, D)), _full((D, D))],
        out_specs=_tok(D),
        compiler_params=_params(),
        name="rwkv_out",
    )(y_ctx, y_lat, bonus, g, x, mods_l, _row(lg), _row(lb), wo)


def _rope_tables():
    t = jnp.arange(DEC_SEQ)
    rowp = (t // GRID_W).astype(F32)
    colp = (t % GRID_W).astype(F32)
    axis_dim = HD // 2
    freqs = ROPE_THETA ** (-jnp.arange(0, axis_dim, 2, dtype=F32) / axis_dim)
    ang = jnp.concatenate([rowp[:, None] * freqs, colp[:, None] * freqs], axis=-1)
    cos = jnp.repeat(jnp.cos(ang), 2, axis=-1)
    sin = jnp.repeat(jnp.sin(ang), 2, axis=-1) * jnp.tile(jnp.array([-1.0, 1.0], F32), HD // 2)
    cos = jnp.concatenate([jnp.ones((TM, HD), F32), cos], axis=0)
    sin = jnp.concatenate([jnp.zeros((TM, HD), F32), sin], axis=0)
    return jnp.tile(cos, (1, LANES // HD)), jnp.tile(sin, (1, LANES // HD))


def _block_diag2(w):
    z = jnp.zeros_like(w[0])
    return jnp.concatenate([jnp.concatenate([w[0], z], axis=1), jnp.concatenate([z, w[1]], axis=1)], axis=0)


def kernel(x_prompt, x_sample, c, c_ctx, cache_k0, cache_v0, cache_k2, cache_v2, state_wkv3,
           norm_g, mod_w, mod_b, ffn_w_in, ffn_w_down, final_norm_g,
           gq_w_qkv, gq_q_norm, gq_k_norm, gq_w_o,
           cv_w_in, cv_b_in, cv_w_dw, cv_b_dw, cv_ln_g, cv_ln_b, cv_w_out, cv_b_out,
           df_w_qkv, df_lambda_q1, df_lambda_k1, df_lambda_q2, df_lambda_k2, df_subln_g, df_w_o,
           rw_mix, rw_w_r, rw_w_k, rw_w_v, rw_w_o, rw_k_k, rw_k_a, rw_r_k, rw_g1, rw_g2,
           rw_ln_g, rw_ln_b, rw_w0, rw_w1, rw_w2, rw_a0, rw_a1, rw_a2):
    bf = lambda w: w.astype(BF16)
    x = jnp.concatenate([x_prompt.reshape(N_CTX, D), x_sample.reshape(N_LAT, D)], axis=0)
    cond8 = jnp.concatenate([c_ctx[None, :], c, jnp.zeros((8 - 1 - DEC_BATCH, D), F32)], axis=0)
    mods = _mods_call(cond8, mod_w, mod_b)
    cos_t, sin_t = _rope_tables()

    def ffn(x, layer, which, final_g=None):
        return _ffn_call(x, mods[layer], norm_g[layer, 2 * which], bf(ffn_w_in[layer, which]),
                         bf(ffn_w_down[layer, which]), 6 * which, final_g)

    x = ffn(x, 0, 0)
    q, k, v = _gqa_proj_call(x, mods[0], norm_g[0, 1], bf(gq_w_qkv), gq_q_norm, gq_k_norm, cos_t, sin_t)
    new_k0 = k[:N_CTX].reshape(BATCH, SEQ, GQ_KV, HD)
    new_v0 = v[:N_CTX].reshape(BATCH, SEQ, GQ_KV, HD)
    oc = _gqa_attn_call(q, k, v)
    ol = _gqa_attn_call(q, k, v, cache_k0.reshape(DEC_BATCH * PAST, GQ_KVW),
                        cache_v0.reshape(DEC_BATCH * PAST, GQ_KVW))
    x = _out_call(oc, ol, x, mods[0], bf(gq_w_o))
    x = ffn(x, 0, 1)

    x = ffn(x, 1, 0)
    u = _conv_in_call(x, mods[1], norm_g[1, 1], bf(cv_w_in), cv_b_in)
    x = _conv_out_call(u, x, mods[1], cv_w_dw, cv_b_dw, cv_ln_g, cv_ln_b, bf(cv_w_out), cv_b_out)
    x = ffn(x, 1, 1)

    x = ffn(x, 2, 0)
    q, k, v = _diff_proj_call(x, mods[2], norm_g[2, 1], bf(df_w_qkv), cos_t, sin_t)
    new_k2 = k[:N_CTX].reshape(BATCH, SEQ, DF_HEADS, 2, HD)
    new_v2 = v[:N_CTX].reshape(BATCH, SEQ, DF_HEADS, 2 * HD)
    lam_rows = jnp.stack([df_lambda_q1, df_lambda_k1, df_lambda_q2, df_lambda_k2]).astype(F32)
    oc = _diff_attn_call(q, k, v, lam_rows, df_subln_g)
    ol = _diff_attn_call(q, k, v, lam_rows, df_subln_g, cache_k2.reshape(DEC_BATCH * PAST, D),
                         cache_v2.reshape(DEC_BATCH * PAST, D))
    x = _out_call(oc, ol, x, mods[2], bf(df_w_o))
    x = ffn(x, 2, 1)

    x = ffn(x, 3, 0)
    w1 = bf(jnp.concatenate([rw_w1[0], rw_w1[1]], axis=1))
    a1 = bf(jnp.concatenate([rw_a1[0], rw_a1[1]], axis=1))
    r, v, kn, g, bonus, lw, kd, al = _rwkv_in_call(
        x, mods[3], norm_g[3, 1], rw_mix, bf(rw_w_r), bf(rw_w_k), bf(rw_w_v), bf(rw_g1), bf(rw_g2),
        w1, bf(_block_diag2(rw_w2)), rw_w0.reshape(-1), a1, bf(_block_diag2(rw_a2)), rw_a0.reshape(-1),
        rw_k_k, rw_k_a, rw_r_k.reshape(-1))
    s0 = state_wkv3.transpose(0, 1, 3, 2, 4).reshape(DEC_BATCH, 2, HD, D)
    y_ctx, s_ctx = _wkv_call(r, v, kn, lw, kd, al, BATCH, SEQ, 0)
    y_lat, _ = _wkv_call(r, v, kn, lw, kd, al, DEC_BATCH, DEC_SEQ, N_CTX, s0)
    new_wkv3 = s_ctx.reshape(BATCH, 2, HD, RW_HEADS, HD).transpose(0, 1, 3, 2, 4)
    x = _rwkv_out_call(y_ctx, y_lat, bonus, g, x, mods[3], rw_ln_g, rw_ln_b, bf(rw_w_o))
    x = ffn(x, 3, 1, final_norm_g)

    y_prompt = x[:N_CTX].reshape(BATCH, SEQ, D)
    y_sample = x[N_CTX:].reshape(DEC_BATCH, DEC_SEQ, D)
    return (y_prompt, y_sample, new_k0, new_v0, new_k2, new_v2, new_wkv3)
```

```python
import functools

import jax
import jax.numpy as jnp
from jax import lax
from jax.experimental import pallas as pl
from jax.experimental.pallas import tpu as pltpu

F32 = jnp.float32
BF16 = jnp.bfloat16

D = 1024
BATCH, SEQ = 32, 256
DEC_BATCH, DEC_SEQ = 2, 2048
PAST = 512
DEPTH = 4
GRID_W = 64
N_MOD = 9
NORM_EPS = 1e-6
LN_EPS = 1e-5
ROPE_THETA = 10000.0
D_FF = 2816
HD = 64
GQ_HEADS, GQ_KV = 16, 4
GQ_GROUP = GQ_HEADS // GQ_KV
GQ_KVW = GQ_KV * HD
CV_WIDTH = 31
CV_PAD = CV_WIDTH // 2
DF_HEADS = 8
DF_LAMBDA_INIT = 0.470713018
DF_SUBLN_EPS = 1e-5
RW_HEADS = 16
RW_GN_EPS = 64e-5
LORA = 64
GATE_LORA = 128

N_CTX = BATCH * SEQ
N_LAT = DEC_BATCH * DEC_SEQ
N_TOK = N_CTX + N_LAT

LANES = 128
TM = 256
TM_FFN = 512
FF_CHUNK = 256
WKV_CHUNK = 64
QUAD = 4 * HD
VMEM_LIMIT = 56 * 1024 * 1024


def _mm(a, b):
    return jnp.dot(a.astype(BF16), b.astype(BF16), preferred_element_type=F32)


def _mm_nt(a, b):
    return lax.dot_general(a.astype(BF16), b.astype(BF16), (((1,), (1,)), ((), ())),
                           preferred_element_type=F32)


def _mm_tn(a, b):
    return lax.dot_general(a.astype(BF16), b.astype(BF16), (((0,), (0,)), ((), ())),
                           preferred_element_type=F32)


def _split3(x):
    hi = x.astype(BF16)
    r1 = x - hi.astype(F32)
    mid = r1.astype(BF16)
    lo = (r1 - mid.astype(F32)).astype(BF16)
    return hi, mid, lo


def _split2(x):
    hi = x.astype(BF16)
    return hi, (x - hi.astype(F32)).astype(BF16)


def _silu(x):
    return x * jax.nn.sigmoid(x)


def _rms(x, g, eps=NORM_EPS):
    return x * lax.rsqrt(jnp.mean(x * x, axis=-1, keepdims=True) + eps) * g


def _seg_ones():
    r = lax.broadcasted_iota(jnp.int32, (LANES, LANES), 0) // HD
    c = lax.broadcasted_iota(jnp.int32, (LANES, LANES), 1) // HD
    return (r == c).astype(BF16)


def _seg_sum(x, ones):
    outs = []
    for g in range(x.shape[1] // LANES):
        hi, mid, lo = _split3(x[:, g * LANES:(g + 1) * LANES])
        dot = functools.partial(jnp.dot, preferred_element_type=F32)
        outs.append(dot(hi, ones) + (dot(mid, ones) + dot(lo, ones)))
    return outs[0] if len(outs) == 1 else jnp.concatenate(outs, axis=1)


def _rope(x, cos, sin_s):
    lane = lax.broadcasted_iota(jnp.int32, (x.shape[0], LANES), 1)
    even = (lane % 2) == 0
    outs = []
    for g in range(x.shape[1] // LANES):
        xg = x[:, g * LANES:(g + 1) * LANES]
        partner = jnp.where(even, pltpu.roll(xg, LANES - 1, 1), pltpu.roll(xg, 1, 1))
        outs.append(xg * cos + partner * sin_s)
    return outs[0] if len(outs) == 1 else jnp.concatenate(outs, axis=1)


def _cond_row(i, tm):
    nt_ctx = N_CTX // tm
    return jnp.where(i < nt_ctx, 0, 1 + (i - nt_ctx) // (DEC_SEQ // tm))


def _tok(width, tm=TM):
    return pl.BlockSpec((tm, width), lambda i: (i, 0))


def _full(shape):
    nd = len(shape)
    return pl.BlockSpec(tuple(shape), lambda i: (0,) * nd, pipeline_mode=pl.Buffered(1))


def _modspec(tm=TM):
    return pl.BlockSpec((1, N_MOD, D), lambda i: (_cond_row(i, tm), 0, 0))


def _ropespec():
    nt_ctx = N_CTX // TM
    per_seq = DEC_SEQ // TM
    return pl.BlockSpec((TM, LANES), lambda i: (jnp.where(i < nt_ctx, 0, 1 + (i - nt_ctx) % per_seq), 0))


def _params(n_axes=1):
    return pltpu.CompilerParams(dimension_semantics=("arbitrary",) * n_axes, vmem_limit_bytes=VMEM_LIMIT)


def _row(v):
    return v.reshape(1, -1).astype(F32)


def _ctx_lat_specs(block, tm=TM):
    nt_ctx = N_CTX // tm
    lead = (0,) * (len(block) - 2)
    return (pl.BlockSpec(block, lambda i: lead + (jnp.minimum(i, nt_ctx - 1), 0)),
            pl.BlockSpec(block, lambda i: lead + (jnp.maximum(i - nt_ctx, 0), 0)))


def _ctx_lat_shapes(width):
    return (jax.ShapeDtypeStruct((N_CTX, width), F32), jax.ShapeDtypeStruct((N_LAT, width), F32))


def _ctx_or_lat(c_ref, l_ref, tm=TM):
    return jnp.where(pl.program_id(0) < N_CTX // tm, c_ref[...], l_ref[...])


def _store_ctx_lat(c_ref, l_ref, val, tm=TM):
    l_ref[...] = val

    @pl.when(pl.program_id(0) < N_CTX // tm)
    def _():
        c_ref[...] = val


def _mods_kernel(c_ref, w_ref, b_ref, o_ref):
    o_ref[0] = _mm(_silu(c_ref[...]), w_ref[0]) + b_ref[0]


def _mods_call(cond8, mod_w, mod_b):
    out = pl.pallas_call(
        _mods_kernel,
        out_shape=jax.ShapeDtypeStruct((DEPTH, 8, N_MOD * D), F32),
        grid=(DEPTH, N_MOD),
        in_specs=[pl.BlockSpec((8, D), lambda l, j: (0, 0)),
                  pl.BlockSpec((1, D, D), lambda l, j: (l, 0, j)),
                  pl.BlockSpec((1, 1, D), lambda l, j: (l, 0, j))],
        out_specs=pl.BlockSpec((1, 8, D), lambda l, j: (l, 0, j)),
        compiler_params=_params(2),
        name="adaln_mods",
    )(cond8, mod_w, mod_b.reshape(DEPTH, 1, N_MOD * D))
    return out.reshape(DEPTH, 8, N_MOD, D)


def _ffn_kernel(*refs, row0, first, final):
    refs = list(refs)
    x = _ctx_or_lat(refs.pop(0), refs.pop(0), TM_FFN) if first else refs.pop(0)[...]
    mod_ref, g_ref, win_ref, wd_ref = refs[:4]
    act_ref = refs[-1]
    m = mod_ref[0]
    h = _rms(x, g_ref[...]) * (1.0 + m[row0 + 1:row0 + 2]) + m[row0:row0 + 1]
    hb = h.astype(BF16)
    for c in range(D_FF // FF_CHUNK):
        lo = c * FF_CHUNK
        gate = jnp.dot(hb, win_ref[:, lo:lo + FF_CHUNK], preferred_element_type=F32)
        up = jnp.dot(hb, win_ref[:, D_FF + lo:D_FF + lo + FF_CHUNK], preferred_element_type=F32)
        act_ref[:, lo:lo + FF_CHUNK] = (_silu(gate) * up).astype(BF16)
    out = jnp.dot(act_ref[...], wd_ref[...], preferred_element_type=F32)
    y = x + (0.5 * m[row0 + 2:row0 + 3]) * out
    if final:
        fg_ref, oc_ref, ol_ref = refs[4:7]
        _store_ctx_lat(oc_ref, ol_ref, _rms(y, fg_ref[...]), TM_FFN)
    else:
        refs[4][...] = y


def _ffn_call(x, mods_l, g, w_in, w_down, layer, which, final_g=None):
    first = isinstance(x, tuple)
    final = final_g is not None
    in_specs = (list(_ctx_lat_specs((TM_FFN, D), TM_FFN)) if first else [_tok(D, TM_FFN)]) + [
        _modspec(TM_FFN), _full((1, D)),
        pl.BlockSpec((None, None, D, 2 * D_FF), lambda i: (layer, which, 0, 0), pipeline_mode=pl.Buffered(1)),
        pl.BlockSpec((None, None, D_FF, D), lambda i: (layer, which, 0, 0), pipeline_mode=pl.Buffered(1))]
    args = (list(x) if first else [x]) + [mods_l, _row(g), w_in, w_down]
    if final:
        in_specs.append(_full((1, D)))
        args.append(_row(final_g))
        out_shape = (jax.ShapeDtypeStruct((N_CTX, D), F32), jax.ShapeDtypeStruct((N_LAT, D), F32))
        out_specs = _ctx_lat_specs((TM_FFN, D), TM_FFN)
    else:
        out_shape = jax.ShapeDtypeStruct((N_TOK, D), F32)
        out_specs = _tok(D, TM_FFN)
    return pl.pallas_call(
        functools.partial(_ffn_kernel, row0=6 * which, first=first, final=final),
        out_shape=out_shape,
        grid=(N_TOK // TM_FFN,),
        in_specs=in_specs,
        out_specs=out_specs,
        scratch_shapes=[pltpu.VMEM((TM_FFN, D_FF), BF16)],
        compiler_params=_params(),
        name="ffn_final" if final else "ffn",
    )(*args)


def _out_kernel(oc_ref, ol_ref, x_ref, mod_ref, w_ref, y_ref):
    out = _mm(_ctx_or_lat(oc_ref, ol_ref), w_ref[...])
    y_ref[...] = x_ref[...] + mod_ref[0][5:6] * out


def _out_call(oc, ol, x, mods_l, w):
    return pl.pallas_call(
        _out_kernel,
        out_shape=jax.ShapeDtypeStruct((N_TOK, D), F32),
        grid=(N_TOK // TM,),
        in_specs=[*_ctx_lat_specs((TM, oc.shape[1])), _tok(D), _modspec(), _full(w.shape)],
        out_specs=_tok(D),
        compiler_params=_params(),
        name="out_proj",
    )(oc, ol, x, mods_l, w)


def _gqa_proj_kernel(x_ref, mod_ref, g_ref, w_ref, qn_ref, kn_ref, cos_ref, sin_ref,
                     q_ref, kc_ref, kl_ref, vc_ref, vl_ref):
    m = mod_ref[0]
    h = _rms(x_ref[...], g_ref[...]) * (1.0 + m[4:5]) + m[3:4]
    qkv = _mm(h, w_ref[...])
    ones = _seg_ones()
    cos, sin_s = cos_ref[...], sin_ref[...]
    q = qkv[:, :D]
    k = qkv[:, D:D + GQ_KVW]
    q = q * lax.rsqrt(_seg_sum(q * q, ones) * (1.0 / HD) + NORM_EPS) * qn_ref[...]
    k = k * lax.rsqrt(_seg_sum(k * k, ones) * (1.0 / HD) + NORM_EPS) * kn_ref[...]
    q_ref[...] = _rope(q, cos, sin_s)
    _store_ctx_lat(kc_ref, kl_ref, _rope(k, cos, sin_s))
    _store_ctx_lat(vc_ref, vl_ref, qkv[:, D + GQ_KVW:])


def _gqa_proj_call(x, mods_l, g, w, qn, kn, cos_t, sin_t):
    return pl.pallas_call(
        _gqa_proj_kernel,
        out_shape=(jax.ShapeDtypeStruct((N_TOK, D), F32),) + _ctx_lat_shapes(GQ_KVW) * 2,
        grid=(N_TOK // TM,),
        in_specs=[_tok(D), _modspec(), _full((1, D)), _full(w.shape), _full((1, D)), _full((1, GQ_KVW)),
                  _ropespec(), _ropespec()],
        out_specs=(_tok(D),) + _ctx_lat_specs((TM, GQ_KVW)) * 2,
        compiler_params=_params(),
        name="gqa_proj",
    )(x, mods_l, _row(g), w, _row(jnp.tile(qn, GQ_HEADS)), _row(jnp.tile(kn, GQ_KV)), cos_t, sin_t)


def _softmax_parts(q, ks):
    ss = [_mm_nt(q, k) for k in ks]
    mx = functools.reduce(jnp.maximum, [jnp.max(s, axis=-1, keepdims=True) for s in ss])
    es = [jnp.exp(s - mx) for s in ss]
    return es, functools.reduce(jnp.add, [jnp.sum(e, axis=-1, keepdims=True) for e in es])


def _attn_specs(latent, kv_width):
    if not latent:
        tok = pl.BlockSpec((SEQ, D), lambda i: (i, 0))
        kv = pl.BlockSpec((SEQ, kv_width), lambda i: (i, 0))
        return (BATCH,), tok, [kv], tok
    per = DEC_SEQ // TM
    q = pl.BlockSpec((TM, D), lambda b, j: (N_CTX // TM + b * per + j, 0))
    cache = pl.BlockSpec((PAST, kv_width), lambda b, j: (b, 0))
    own = pl.BlockSpec((DEC_SEQ, kv_width), lambda b, j: (b, 0))
    out = pl.BlockSpec((TM, D), lambda b, j: (b * per + j, 0))
    return (DEC_BATCH, per), q, [cache, own], out


def _gqa_attn_kernel(q_ref, *refs):
    o_ref = refs[-1]
    k_refs, v_refs = refs[0:-1:2], refs[1:-1:2]
    scale = HD ** -0.5
    for kv in range(GQ_KV):
        cols = slice(kv * HD, (kv + 1) * HD)
        ks = [r[:, cols] for r in k_refs]
        vs = [r[:, cols].astype(BF16) for r in v_refs]
        for g in range(GQ_GROUP):
            h = kv * GQ_GROUP + g
            es, l = _softmax_parts(q_ref[:, h * HD:(h + 1) * HD] * scale, ks)
            pv = functools.reduce(jnp.add, [_mm(e, v) for e, v in zip(es, vs)])
            o_ref[:, h * HD:(h + 1) * HD] = pv / l


def _gqa_attn_call(q, k, v, cache_k=None, cache_v=None):
    latent = cache_k is not None
    grid, q_spec, kv_specs, o_spec = _attn_specs(latent, GQ_KVW)
    parts = [(cache_k, cache_v), (k, v)] if latent else [(k, v)]
    return pl.pallas_call(
        _gqa_attn_kernel,
        out_shape=jax.ShapeDtypeStruct((N_LAT if latent else N_CTX, D), F32),
        grid=grid,
        in_specs=[q_spec] + [s for s in kv_specs for _ in range(2)],
        out_specs=o_spec,
        compiler_params=_params(len(grid)),
        name="gqa_attn",
    )(q, *[a for part in parts for a in part])


def _conv_in_kernel(x_ref, mod_ref, g_ref, w_ref, b_ref, u_ref):
    m = mod_ref[0]
    h = _rms(x_ref[...], g_ref[...]) * (1.0 + m[4:5]) + m[3:4]
    u = _mm(h, w_ref[...]) + b_ref[...]
    u_ref[...] = u[:, :D] * jax.nn.sigmoid(u[:, D:])


def _conv_in_call(x, mods_l, g, w, b):
    return pl.pallas_call(
        _conv_in_kernel,
        out_shape=jax.ShapeDtypeStruct((N_TOK, D), F32),
        grid=(N_TOK // TM,),
        in_specs=[_tok(D), _modspec(), _full((1, D)), _full(w.shape), _full((1, 2 * D))],
        out_specs=_tok(D),
        compiler_params=_params(),
        name="conv_in",
    )(x, mods_l, _row(g), w, _row(b))


HALO = 16


def _seq_edges(i):
    nt_ctx = N_CTX // TM
    per = jnp.where(i < nt_ctx, SEQ // TM, DEC_SEQ // TM)
    j = jnp.where(i < nt_ctx, i, i - nt_ctx) % per
    return j == 0, j == per - 1


def _conv_out_kernel(u_ref, up_ref, un_ref, x_ref, mod_ref, wdw_ref, bdw_ref, lg_ref, lb_ref, wo_ref, bo_ref,
                     y_ref, pad_ref):
    start, end = _seq_edges(pl.program_id(0))
    pad_ref[0:HALO, :] = jnp.where(start, 0.0, up_ref[...])
    pad_ref[HALO:HALO + TM, :] = u_ref[...]
    pad_ref[HALO + TM:, :] = jnp.where(end, 0.0, un_ref[...])
    wdw = wdw_ref[...]
    acc = jnp.zeros((TM, D), F32)
    for j in range(CV_WIDTH):
        off = HALO - CV_PAD + j
        acc = acc + wdw[j:j + 1] * pad_ref[off:off + TM, :]
    u = acc + bdw_ref[...]
    mu = jnp.mean(u, axis=-1, keepdims=True)
    var = jnp.mean(jnp.square(u - mu), axis=-1, keepdims=True)
    u = _silu((u - mu) * lax.rsqrt(var + LN_EPS) * lg_ref[...] + lb_ref[...])
    out = _mm(u, wo_ref[...]) + bo_ref[...]
    y_ref[...] = x_ref[...] + mod_ref[0][5:6] * out


def _conv_out_call(u, x, mods_l, wdw, bdw, lg, lb, wo, bo):
    per = TM // HALO
    last = N_TOK // HALO - 1
    return pl.pallas_call(
        _conv_out_kernel,
        out_shape=jax.ShapeDtypeStruct((N_TOK, D), F32),
        grid=(N_TOK // TM,),
        in_specs=[_tok(D),
                  pl.BlockSpec((HALO, D), lambda i: (jnp.maximum(i * per - 1, 0), 0)),
                  pl.BlockSpec((HALO, D), lambda i: (jnp.minimum((i + 1) * per, last), 0)),
                  _tok(D), _modspec(), _full((CV_WIDTH, D)), _full((1, D)), _full((1, D)), _full((1, D)),
                  _full((D, D)), _full((1, D))],
        out_specs=_tok(D),
        scratch_shapes=[pltpu.VMEM((TM + 2 * HALO, D), F32)],
        compiler_params=_params(),
        name="conv_out",
    )(u, u, u, x, mods_l, wdw, _row(bdw), _row(lg), _row(lb), wo, _row(bo))


def _diff_proj_kernel(x_ref, mod_ref, g_ref, w_ref, cos_ref, sin_ref, q_ref, kc_ref, kl_ref, vc_ref, vl_ref):
    m = mod_ref[0]
    h = _rms(x_ref[...], g_ref[...]) * (1.0 + m[4:5]) + m[3:4]
    qkv = _mm(h, w_ref[...])
    cos, sin_s = cos_ref[...], sin_ref[...]
    q_ref[...] = _rope(qkv[:, :D], cos, sin_s)
    _store_ctx_lat(kc_ref, kl_ref, _rope(qkv[:, D:2 * D], cos, sin_s))
    _store_ctx_lat(vc_ref, vl_ref, qkv[:, 2 * D:])


def _diff_proj_call(x, mods_l, g, w, cos_t, sin_t):
    return pl.pallas_call(
        _diff_proj_kernel,
        out_shape=(jax.ShapeDtypeStruct((N_TOK, D), F32),) + _ctx_lat_shapes(D) * 2,
        grid=(N_TOK // TM,),
        in_specs=[_tok(D), _modspec(), _full((1, D)), _full(w.shape), _ropespec(), _ropespec()],
        out_specs=(_tok(D),) + _ctx_lat_specs((TM, D)) * 2,
        compiler_params=_params(),
        name="diff_proj",
    )(x, mods_l, _row(g), w, cos_t, sin_t)


def _diff_attn_kernel(q_ref, lam_ref, sg_ref, *refs):
    o_ref = refs[-1]
    k_refs, v_refs = refs[0:-1:2], refs[1:-1:2]
    lv = lam_ref[...]
    lam = (jnp.exp(jnp.sum(lv[0:1] * lv[1:2], axis=-1, keepdims=True))
           - jnp.exp(jnp.sum(lv[2:3] * lv[3:4], axis=-1, keepdims=True)) + DF_LAMBDA_INIT)
    scale = HD ** -0.5
    for h in range(DF_HEADS):
        c1, c2 = (2 * h) * HD, (2 * h + 1) * HD
        es1, l1 = _softmax_parts(q_ref[:, c1:c1 + HD] * scale, [r[:, c1:c1 + HD] for r in k_refs])
        es2, l2 = _softmax_parts(q_ref[:, c2:c2 + HD] * scale, [r[:, c2:c2 + HD] for r in k_refs])
        inv1, inv2 = 1.0 / l1, lam / l2
        o = functools.reduce(jnp.add, [_mm(e1 * inv1 - e2 * inv2, r[:, c1:c1 + 2 * HD])
                                       for e1, e2, r in zip(es1, es2, v_refs)])
        o = _rms(o, sg_ref[...], DF_SUBLN_EPS) * (1.0 - DF_LAMBDA_INIT)
        o_ref[:, c1:c1 + 2 * HD] = o


def _diff_attn_call(q, k, v, lam_rows, subln_g, cache_k=None, cache_v=None):
    latent = cache_k is not None
    grid, q_spec, kv_specs, o_spec = _attn_specs(latent, D)
    parts = [(cache_k, cache_v), (k, v)] if latent else [(k, v)]
    nd = len(grid)
    return pl.pallas_call(
        _diff_attn_kernel,
        out_shape=jax.ShapeDtypeStruct((N_LAT if latent else N_CTX, D), F32),
        grid=grid,
        in_specs=[q_spec, pl.BlockSpec((4, HD), lambda *_: (0, 0)), pl.BlockSpec((1, 2 * HD), lambda *_: (0, 0))]
        + [s for s in kv_specs for _ in range(2)],
        out_specs=o_spec,
        compiler_params=_params(nd),
        name="diff_attn",
    )(q, lam_rows, _row(subln_g), *[a for part in parts for a in part])


def _rwkv_in_kernel(x_ref, xp_ref, xn_ref, mod_ref, g_ref, mix_ref, wr_ref, wk_ref, wv_ref, g1_ref, g2_ref,
                    w1_ref, w2_ref, w0_ref, a1_ref, a2_ref, a0_ref, kk_ref, ka_ref, rk_ref,
                    r_out, v_out, kn_out, g_out, bonus_out, lw_out, kd_out, al_out):
    m = mod_ref[0]
    gn = g_ref[...]

    def norm_mod(z):
        return _rms(z, gn) * (1.0 + m[4:5]) + m[3:4]

    start, end = _seq_edges(pl.program_id(0))
    h = norm_mod(x_ref[...])
    h_prev = jnp.where(start, 0.0, norm_mod(xp_ref[...])[7:8])
    h_next = jnp.where(end, 0.0, norm_mod(xn_ref[...])[0:1])
    row = lax.broadcasted_iota(jnp.int32, (TM, D), 0)
    h_dn = jnp.where(row == 0, h_prev, pltpu.roll(h, 1, 0))
    h_up = jnp.where(row == TM - 1, h_next, pltpu.roll(h, TM - 1, 0))
    xx = 0.5 * (h_dn + h_up) - h
    mix = mix_ref[...]
    xr, xw, xk, xv, xa, xg = (h + xx * mix[i:i + 1] for i in range(6))

    r = _mm(xr, wr_ref[...])
    k = _mm(xk, wk_ref[...])
    v = _mm(xv, wv_ref[...])
    g = _mm(jax.nn.sigmoid(_mm(xg, g1_ref[...])), g2_ref[...])
    wl = w0_ref[...] + _mm(jnp.tanh(_mm(xw, w1_ref[...])), w2_ref[...])
    softplus = jnp.maximum(-wl, 0.0) + jnp.log(1.0 + jnp.exp(-jnp.abs(wl)))
    log_decay = -jnp.exp(-softplus - 0.5)
    alpha = jax.nn.sigmoid(a0_ref[...] + _mm(_mm(xa, a1_ref[...]), a2_ref[...]))

    ones = _seg_ones()
    kk = k * kk_ref[...]
    kk = kk * lax.rsqrt(jnp.maximum(_seg_sum(kk * kk, ones), 1e-24))
    r_out[...] = r
    v_out[...] = v
    kn_out[...] = kk
    g_out[...] = g
    bonus = jnp.zeros((TM, D), F32)
    for d in range(2):
        al = alpha[:, d * D:(d + 1) * D]
        kd = k * (1.0 + (al - 1.0) * ka_ref[...])
        bonus = bonus + _seg_sum(r * kd * rk_ref[...], ones) * v
        lw_out[d] = log_decay[:, d * D:(d + 1) * D]
        kd_out[d] = kd
        al_out[d] = al
    bonus_out[...] = bonus


def _rwkv_in_call(x, mods_l, g, mix, wr, wk, wv, g1, g2, w1, w2, w0, a1, a2, a0, k_k, k_a, r_k):
    per = TM // 8
    last = N_TOK // 8 - 1
    tok_out = jax.ShapeDtypeStruct((N_TOK, D), F32)
    dir_out = jax.ShapeDtypeStruct((2, N_TOK, D), F32)
    dir_spec = pl.BlockSpec((2, TM, D), lambda i: (0, i, 0))
    return pl.pallas_call(
        _rwkv_in_kernel,
        out_shape=(tok_out,) * 5 + (dir_out,) * 3,
        grid=(N_TOK // TM,),
        in_specs=[_tok(D),
                  pl.BlockSpec((8, D), lambda i: (jnp.maximum(i * per - 1, 0), 0)),
                  pl.BlockSpec((8, D), lambda i: (jnp.minimum((i + 1) * per, last), 0)),
                  _modspec(), _full((1, D)), _full((6, D)),
                  _full((D, D)), _full((D, D)), _full((D, D)), _full((D, GATE_LORA)), _full((GATE_LORA, D)),
                  _full((D, 2 * LORA)), _full((2 * LORA, 2 * D)), _full((1, 2 * D)),
                  _full((D, 2 * LORA)), _full((2 * LORA, 2 * D)), _full((1, 2 * D)),
                  _full((1, D)), _full((1, D)), _full((1, D))],
        out_specs=(_tok(D),) * 5 + (dir_spec,) * 3,
        compiler_params=_params(),
        name="rwkv_in",
    )(x, x, x, mods_l, _row(g), mix, wr, wk, wv, g1, g2, w1, w2, _row(w0), a1, a2, _row(a0),
      _row(k_k), _row(k_a), _row(r_k))


def _block_diag_rows(z):
    zb = z.astype(BF16)
    lane_head = lax.broadcasted_iota(jnp.int32, zb.shape, 1) // HD
    zero = jnp.zeros_like(zb)
    return jnp.concatenate([jnp.where(lane_head == h, zb, zero) for h in range(QUAD // HD)], axis=0)


def _wkv_prep(rev, r_ref, v_ref, kn_ref, lw_ref, kd_ref, al_ref):
    C = WKV_CHUNK
    row = lax.broadcasted_iota(jnp.int32, (C, C), 0)
    col = lax.broadcasted_iota(jnp.int32, (C, C), 1)
    incl = (col >= row) if rev else (col <= row)
    lw = lw_ref[0]
    hi, mid, lo = _split3(lw)
    inc_b = incl.astype(BF16)
    dot = functools.partial(jnp.dot, preferred_element_type=F32)
    g_incl = dot(inc_b, hi) + (dot(inc_b, mid) + dot(inc_b, lo))
    g_tot = jnp.sum(lw, axis=0, keepdims=True)
    kn = kn_ref[...]
    b = kn * al_ref[0]
    kd = kd_ref[0]
    e_neg = jnp.exp(-g_incl)
    e_rem = jnp.exp(g_tot - g_incl)
    return dict(a_t=-kn * jnp.exp(g_incl - lw), r_t=r_ref[...] * jnp.exp(g_incl), b_t=b * e_neg, k_t=kd * e_neg,
                b_h=b * e_rem, k_h=kd * e_rem, w_tot=jnp.exp(g_tot), v=v_ref[...])


def _wkv_kernel(*refs, n_chunks, has_init):
    fwd_refs, bwd_refs, rest = refs[0:6], refs[6:12], refs[12:]
    if has_init:
        s0_ref, yf_ref, yb_ref, sfin_ref, s_ref = rest
    else:
        yf_ref, yb_ref, sfin_ref, s_ref = rest
    c = pl.program_id(1)
    C = WKV_CHUNK

    @pl.when(c == 0)
    def _():
        if has_init:
            for d in range(2):
                for h in range(RW_HEADS):
                    s_ref[d, :, h * HD:(h + 1) * HD] = s0_ref[0, d, h]
        else:
            s_ref[...] = jnp.zeros_like(s_ref)

    row4 = lax.broadcasted_iota(jnp.int32, (C, QUAD), 0)
    col4 = lax.broadcasted_iota(jnp.int32, (C, QUAD), 1) % C
    eye4 = (col4 == row4).astype(F32)
    r2 = lax.broadcasted_iota(jnp.int32, (QUAD, QUAD), 0)
    c2 = lax.broadcasted_iota(jnp.int32, (QUAD, QUAD), 1)
    same_head = (r2 // HD) == (c2 // HD)
    strict4 = (col4 < row4, col4 > row4)
    incl4 = (col4 <= row4, col4 >= row4)

    prep = (_wkv_prep(False, *fwd_refs), _wkv_prep(True, *bwd_refs))
    y_refs = (yf_ref, yb_ref)
    chains = [(d, slice(q * QUAD, (q + 1) * QUAD)) for d in range(2) for q in range(D // QUAD)]

    def rows(name):
        return [prep[d][name][:, sl] for d, sl in chains]

    a_t, r_t, b_t, k_t, b_h, k_h, w_tot, v = (rows(n) for n in ("a_t", "r_t", "b_t", "k_t", "b_h", "k_h", "w_tot", "v"))
    s_in = [s_ref[d, :, sl] for d, sl in chains]
    bd = _block_diag_rows
    dot = functools.partial(jnp.dot, preferred_element_type=F32)
    cat = functools.partial(jnp.concatenate, axis=0)
    ar = [cat([a, r]) for a, r in zip(a_t, r_t)]
    gram_b = [_mm_nt(z, bd(b)) for z, b in zip(ar, b_t)]
    gram_k = [_mm_nt(z, bd(k)) for z, k in zip(ar, k_t)]
    a_ab = [jnp.where(strict4[d], g[:C], 0.0) for (d, _), g in zip(chains, gram_b)]
    a_ak = [jnp.where(strict4[d], g[:C], 0.0) for (d, _), g in zip(chains, gram_k)]
    a_rb = [jnp.where(incl4[d], g[C:], 0.0) for (d, _), g in zip(chains, gram_b)]
    a_rk = [jnp.where(incl4[d], g[C:], 0.0) for (d, _), g in zip(chains, gram_k)]
    n_steps = 6
    p = a_ab
    x = [eye4 + l for l in a_ab]
    for i in range(n_steps):
        ph, plo = zip(*[_split2(pq) for pq in p])
        wh = [bd(z) for z in ph]
        wl = [bd(z) for z in plo]
        square = i < n_steps - 1
        if i == 0:
            r1 = [dot(cat([h, l]), w) for h, l, w in zip(ph, plo, wh)]
            r2 = [dot(h, w) for h, w in zip(ph, wl)]
            p = [a[:C] + (b_ + a[C:]) for a, b_ in zip(r1, r2)]
            continue
        xh, xlo = zip(*[_split2(xq) for xq in x])
        if square:
            r1 = [dot(cat([h, l, g, m]), w) for h, l, g, m, w in zip(ph, plo, xh, xlo, wh)]
            r2 = [dot(cat([h, g]), w) for h, g, w in zip(ph, xh, wl)]
            p = [a[:C] + (b_[:C] + a[C:2 * C]) for a, b_ in zip(r1, r2)]
            x = [xq + (a[2 * C:3 * C] + (b_[C:] + a[3 * C:])) for xq, a, b_ in zip(x, r1, r2)]
        else:
            r1 = [dot(cat([g, m]), w) for g, m, w in zip(xh, xlo, wh)]
            r2 = [dot(g, w) for g, w in zip(xh, wl)]
            x = [xq + (a[:C] + (b_ + a[C:])) for xq, a, b_ in zip(x, r1, r2)]
    vbd = [bd(z) for z in v]
    rv = [_mm(cat([m, n]), vb) for m, n, vb in zip(a_ak, a_rk, vbd)]
    akv = [z[:C] for z in rv]
    wm = [_mm(xq, bd(a)) for xq, a in zip(x, a_t)]
    um = [_mm(xq, bd(z)) for xq, z in zip(x, akv)]
    qm = [r + _mm(m, bd(w)) for r, m, w in zip(r_t, a_rb, wm)]
    y0 = [_mm(m, bd(z)) + z2[C:] for m, z, z2 in zip(a_rb, um, rv)]
    rs = [_mm_nt(cat([w, qq]), bd(sq)) for w, qq, sq in zip(wm, qm, s_in)]
    u = [z[:C] + z2 for z, z2 in zip(rs, um)]
    for (d, sl), z, yy in zip(chains, rs, y0):
        y_refs[d][:, sl] = z[C:] + yy
    for (d, sl), uq, vq, bq, kq, wq, sq in zip(chains, u, v, b_h, k_h, w_tot, s_in):
        upd = _mm_tn(jnp.concatenate([uq, vq], axis=0), jnp.concatenate([bq, kq], axis=0))
        upd = jnp.where(same_head, upd, 0.0)
        upd = (upd[0:HD] + upd[HD:2 * HD]) + (upd[2 * HD:3 * HD] + upd[3 * HD:])
        s_ref[d, :, sl] = sq * wq + upd

    @pl.when(c == n_chunks - 1)
    def _():
        for d in range(2):
            for h in range(RW_HEADS):
                sfin_ref[0, d, h] = s_ref[d, :, h * HD:(h + 1) * HD]


def _wkv_call(r, v, kn, lw, kd, al, n_seq, seq_len, row_off, s0=None):
    C = WKV_CHUNK
    n_chunks = seq_len // C
    off = row_off // C

    def fwd(s, c):
        return s * n_chunks + c

    def bwd(s, c):
        return s * n_chunks + n_chunks - 1 - c

    def specs(blk, d):
        tok = pl.BlockSpec((C, D), lambda s, c: (off + blk(s, c), 0))
        dirtok = pl.BlockSpec((1, C, D), lambda s, c: (d, off + blk(s, c), 0))
        return [tok, tok, tok, dirtok, dirtok, dirtok]

    state = pl.BlockSpec((1, 2, RW_HEADS, HD, HD), lambda s, c: (s, 0, 0, 0, 0))
    has_init = s0 is not None
    in_specs = specs(fwd, 0) + specs(bwd, 1) + ([state] if has_init else [])
    args = [r, v, kn, lw, kd, al] * 2 + ([s0] if has_init else [])
    y_shape = jax.ShapeDtypeStruct((n_seq * seq_len, D), F32)
    return pl.pallas_call(
        functools.partial(_wkv_kernel, n_chunks=n_chunks, has_init=has_init),
        out_shape=(y_shape, y_shape, jax.ShapeDtypeStruct((n_seq, 2, RW_HEADS, HD, HD), F32)),
        grid=(n_seq, n_chunks),
        in_specs=in_specs,
        out_specs=(pl.BlockSpec((C, D), lambda s, c: (fwd(s, c), 0)),
                   pl.BlockSpec((C, D), lambda s, c: (bwd(s, c), 0)), state),
        scratch_shapes=[pltpu.VMEM((2, HD, D), F32)],
        compiler_params=_params(2),
        name="wkv_scan",
    )(*args)


def _rwkv_out_kernel(yfc_ref, yfl_ref, ybc_ref, ybl_ref, bonus_ref, g_ref, x_ref, mod_ref, lg_ref, lb_ref, wo_ref,
                     o_ref):
    ones = _seg_ones()
    y = _ctx_or_lat(yfc_ref, yfl_ref) + _ctx_or_lat(ybc_ref, ybl_ref)
    mu = _seg_sum(y, ones) * (1.0 / HD)
    yc = y - mu
    var = _seg_sum(yc * yc, ones) * (1.0 / HD)
    out = yc * lax.rsqrt(var + RW_GN_EPS) * lg_ref[...] + lb_ref[...] + bonus_ref[...]
    o_ref[...] = x_ref[...] + mod_ref[0][5:6] * _mm(out * g_ref[...], wo_ref[...])


def _rwkv_out_call(y_ctx, y_lat, bonus, g, x, mods_l, lg, lb, wo):
    return pl.pallas_call(
        _rwkv_out_kernel,
        out_shape=jax.ShapeDtypeStruct((N_TOK, D), F32),
        grid=(N_TOK // TM,),
        in_specs=[*_ctx_lat_specs((TM, D)), *_ctx_lat_specs((TM, D)), _tok(D), _tok(D), _tok(D), _modspec(),
                  _full((1, D)), _full((1, D)), _full((D, D))],
        out_specs=_tok(D),
        compiler_params=_params(),
        name="rwkv_out",
    )(y_ctx[0], y_lat[0], y_ctx[1], y_lat[1], bonus, g, x, mods_l, _row(lg), _row(lb), wo)


def _rope_tables():
    t = jnp.arange(DEC_SEQ)
    rowp = (t // GRID_W).astype(F32)
    colp = (t % GRID_W).astype(F32)
    axis_dim = HD // 2
    freqs = ROPE_THETA ** (-jnp.arange(0, axis_dim, 2, dtype=F32) / axis_dim)
    ang = jnp.concatenate([rowp[:, None] * freqs, colp[:, None] * freqs], axis=-1)
    cos = jnp.repeat(jnp.cos(ang), 2, axis=-1)
    sin = jnp.repeat(jnp.sin(ang), 2, axis=-1) * jnp.tile(jnp.array([-1.0, 1.0], F32), HD // 2)
    cos = jnp.concatenate([jnp.ones((TM, HD), F32), cos], axis=0)
    sin = jnp.concatenate([jnp.zeros((TM, HD), F32), sin], axis=0)
    return jnp.tile(cos, (1, LANES // HD)), jnp.tile(sin, (1, LANES // HD))


def _block_diag2(w):
    z = jnp.zeros_like(w[0])
    return jnp.concatenate([jnp.concatenate([w[0], z], axis=1), jnp.concatenate([z, w[1]], axis=1)], axis=0)


def kernel(x_prompt, x_sample, c, c_ctx, cache_k0, cache_v0, cache_k2, cache_v2, state_wkv3,
           norm_g, mod_w, mod_b, ffn_w_in, ffn_w_down, final_norm_g,
           gq_w_qkv, gq_q_norm, gq_k_norm, gq_w_o,
           cv_w_in, cv_b_in, cv_w_dw, cv_b_dw, cv_ln_g, cv_ln_b, cv_w_out, cv_b_out,
           df_w_qkv, df_lambda_q1, df_lambda_k1, df_lambda_q2, df_lambda_k2, df_subln_g, df_w_o,
           rw_mix, rw_w_r, rw_w_k, rw_w_v, rw_w_o, rw_k_k, rw_k_a, rw_r_k, rw_g1, rw_g2,
           rw_ln_g, rw_ln_b, rw_w0, rw_w1, rw_w2, rw_a0, rw_a1, rw_a2):
    bf = lambda w: w.astype(BF16)
    cond8 = jnp.concatenate([c_ctx[None, :], c, jnp.zeros((8 - 1 - DEC_BATCH, D), F32)], axis=0)
    mods = _mods_call(cond8, mod_w, mod_b)
    cos_t, sin_t = _rope_tables()

    w_in_all, w_down_all = bf(ffn_w_in), bf(ffn_w_down)

    def ffn(x, layer, which, final_g=None):
        return _ffn_call(x, mods[layer], norm_g[layer, 2 * which], w_in_all, w_down_all, layer, which, final_g)

    x = ffn((x_prompt.reshape(N_CTX, D), x_sample.reshape(N_LAT, D)), 0, 0)
    q, kc, kl, vc, vl = _gqa_proj_call(x, mods[0], norm_g[0, 1], bf(gq_w_qkv), gq_q_norm, gq_k_norm, cos_t, sin_t)
    new_k0 = kc.reshape(BATCH, SEQ, GQ_KV, HD)
    new_v0 = vc.reshape(BATCH, SEQ, GQ_KV, HD)
    oc = _gqa_attn_call(q, kc, vc)
    ol = _gqa_attn_call(q, kl, vl, cache_k0.reshape(DEC_BATCH * PAST, GQ_KVW),
                        cache_v0.reshape(DEC_BATCH * PAST, GQ_KVW))
    x = _out_call(oc, ol, x, mods[0], bf(gq_w_o))
    x = ffn(x, 0, 1)

    x = ffn(x, 1, 0)
    u = _conv_in_call(x, mods[1], norm_g[1, 1], bf(cv_w_in), cv_b_in)
    x = _conv_out_call(u, x, mods[1], cv_w_dw, cv_b_dw, cv_ln_g, cv_ln_b, bf(cv_w_out), cv_b_out)
    x = ffn(x, 1, 1)

    x = ffn(x, 2, 0)
    q, kc, kl, vc, vl = _diff_proj_call(x, mods[2], norm_g[2, 1], bf(df_w_qkv), cos_t, sin_t)
    new_k2 = kc.reshape(BATCH, SEQ, DF_HEADS, 2, HD)
    new_v2 = vc.reshape(BATCH, SEQ, DF_HEADS, 2 * HD)
    lam_rows = jnp.stack([df_lambda_q1, df_lambda_k1, df_lambda_q2, df_lambda_k2]).astype(F32)
    oc = _diff_attn_call(q, kc, vc, lam_rows, df_subln_g)
    ol = _diff_attn_call(q, kl, vl, lam_rows, df_subln_g, cache_k2.reshape(DEC_BATCH * PAST, D),
                         cache_v2.reshape(DEC_BATCH * PAST, D))
    x = _out_call(oc, ol, x, mods[2], bf(df_w_o))
    x = ffn(x, 2, 1)

    x = ffn(x, 3, 0)
    w1 = bf(jnp.concatenate([rw_w1[0], rw_w1[1]], axis=1))
    a1 = bf(jnp.concatenate([rw_a1[0], rw_a1[1]], axis=1))
    r, v, kn, g, bonus, lw, kd, al = _rwkv_in_call(
        x, mods[3], norm_g[3, 1], rw_mix, bf(rw_w_r), bf(rw_w_k), bf(rw_w_v), bf(rw_g1), bf(rw_g2),
        w1, bf(_block_diag2(rw_w2)), rw_w0.reshape(-1), a1, bf(_block_diag2(rw_a2)), rw_a0.reshape(-1),
        rw_k_k, rw_k_a, rw_r_k.reshape(-1))
    *y_ctx, new_wkv3 = _wkv_call(r, v, kn, lw, kd, al, BATCH, SEQ, 0)
    *y_lat, _ = _wkv_call(r, v, kn, lw, kd, al, DEC_BATCH, DEC_SEQ, N_CTX, state_wkv3)
    x = _rwkv_out_call(y_ctx, y_lat, bonus, g, x, mods[3], rw_ln_g, rw_ln_b, bf(rw_w_o))
    y_prompt, y_sample = ffn(x, 3, 1, final_norm_g)
    y_prompt = y_prompt.reshape(BATCH, SEQ, D)
    y_sample = y_sample.reshape(DEC_BATCH, DEC_SEQ, D)
    return (y_prompt, y_sample, new_k0, new_v0, new_k2, new_v2, new_wkv3)
```

```python
import functools

import jax
import jax.numpy as jnp
from jax import lax
from jax.experimental import pallas as pl
from jax.experimental.pallas import tpu as pltpu

F32 = jnp.float32
BF16 = jnp.bfloat16

D = 1024
BATCH, SEQ = 32, 256
DEC_BATCH, DEC_SEQ = 2, 2048
PAST = 512
DEPTH = 4
GRID_W = 64
N_MOD = 9
NORM_EPS = 1e-6
LN_EPS = 1e-5
ROPE_THETA = 10000.0
D_FF = 2816
HD = 64
GQ_HEADS, GQ_KV = 16, 4
GQ_GROUP = GQ_HEADS // GQ_KV
GQ_KVW = GQ_KV * HD
CV_WIDTH = 31
CV_PAD = CV_WIDTH // 2
DF_HEADS = 8
DF_LAMBDA_INIT = 0.470713018
DF_SUBLN_EPS = 1e-5
RW_HEADS = 16
RW_GN_EPS = 64e-5
LORA = 64
GATE_LORA = 128

N_CTX = BATCH * SEQ
N_LAT = DEC_BATCH * DEC_SEQ
N_TOK = N_CTX + N_LAT

LANES = 128
TM = 256
TM_FFN = 512
FF_CHUNK = 256
WKV_CHUNK = 64
QUAD = 4 * HD
VMEM_LIMIT = 56 * 1024 * 1024


def _mm(a, b):
    return jnp.dot(a.astype(BF16), b.astype(BF16), preferred_element_type=F32)


def _mm_nt(a, b):
    return lax.dot_general(a.astype(BF16), b.astype(BF16), (((1,), (1,)), ((), ())),
                           preferred_element_type=F32)


def _mm_tn(a, b):
    return lax.dot_general(a.astype(BF16), b.astype(BF16), (((0,), (0,)), ((), ())),
                           preferred_element_type=F32)


def _split3(x):
    hi = x.astype(BF16)
    r1 = x - hi.astype(F32)
    mid = r1.astype(BF16)
    lo = (r1 - mid.astype(F32)).astype(BF16)
    return hi, mid, lo


def _split2(x):
    hi = x.astype(BF16)
    return hi, (x - hi.astype(F32)).astype(BF16)


def _silu(x):
    return x * jax.nn.sigmoid(x)


def _rms(x, g, eps=NORM_EPS):
    return x * lax.rsqrt(jnp.mean(x * x, axis=-1, keepdims=True) + eps) * g


def _seg_ones():
    r = lax.broadcasted_iota(jnp.int32, (LANES, LANES), 0) // HD
    c = lax.broadcasted_iota(jnp.int32, (LANES, LANES), 1) // HD
    return (r == c).astype(BF16)


def _seg_sum(x, ones):
    outs = []
    for g in range(x.shape[1] // LANES):
        hi, lo = _split2(x[:, g * LANES:(g + 1) * LANES])
        dot = functools.partial(jnp.dot, preferred_element_type=F32)
        outs.append(dot(hi, ones) + dot(lo, ones))
    return outs[0] if len(outs) == 1 else jnp.concatenate(outs, axis=1)


def _rope(x, cos, sin_s):
    lane = lax.broadcasted_iota(jnp.int32, (x.shape[0], LANES), 1)
    even = (lane % 2) == 0
    outs = []
    for g in range(x.shape[1] // LANES):
        xg = x[:, g * LANES:(g + 1) * LANES]
        partner = jnp.where(even, pltpu.roll(xg, LANES - 1, 1), pltpu.roll(xg, 1, 1))
        outs.append(xg * cos + partner * sin_s)
    return outs[0] if len(outs) == 1 else jnp.concatenate(outs, axis=1)


def _cond_row(i, tm):
    nt_ctx = N_CTX // tm
    return jnp.where(i < nt_ctx, 0, 1 + (i - nt_ctx) // (DEC_SEQ // tm))


def _tok(width, tm=TM):
    return pl.BlockSpec((tm, width), lambda i: (i, 0))


def _full(shape):
    nd = len(shape)
    return pl.BlockSpec(tuple(shape), lambda i: (0,) * nd, pipeline_mode=pl.Buffered(1))


def _modspec(tm=TM):
    return pl.BlockSpec((1, N_MOD, D), lambda i: (_cond_row(i, tm), 0, 0))


def _ropespec():
    nt_ctx = N_CTX // TM
    per_seq = DEC_SEQ // TM
    return pl.BlockSpec((TM, LANES), lambda i: (jnp.where(i < nt_ctx, 0, 1 + (i - nt_ctx) % per_seq), 0))


def _params(n_axes=1):
    return pltpu.CompilerParams(dimension_semantics=("arbitrary",) * n_axes, vmem_limit_bytes=VMEM_LIMIT)


def _row(v):
    return v.reshape(1, -1).astype(F32)


def _ctx_lat_specs(block, tm=TM):
    nt_ctx = N_CTX // tm
    lead = (0,) * (len(block) - 2)
    return (pl.BlockSpec(block, lambda i: lead + (jnp.minimum(i, nt_ctx - 1), 0)),
            pl.BlockSpec(block, lambda i: lead + (jnp.maximum(i - nt_ctx, 0), 0)))


def _ctx_lat_shapes(width):
    return (jax.ShapeDtypeStruct((N_CTX, width), F32), jax.ShapeDtypeStruct((N_LAT, width), F32))


def _ctx_or_lat(c_ref, l_ref, tm=TM):
    return jnp.where(pl.program_id(0) < N_CTX // tm, c_ref[...], l_ref[...])


def _store_ctx_lat(c_ref, l_ref, val, tm=TM):
    l_ref[...] = val

    @pl.when(pl.program_id(0) < N_CTX // tm)
    def _():
        c_ref[...] = val


def _mods_kernel(c_ref, w_ref, b_ref, o_ref):
    o_ref[0] = _mm(_silu(c_ref[...]), w_ref[0]) + b_ref[0]


def _mods_call(cond8, mod_w, mod_b):
    out = pl.pallas_call(
        _mods_kernel,
        out_shape=jax.ShapeDtypeStruct((DEPTH, 8, N_MOD * D), F32),
        grid=(DEPTH, N_MOD),
        in_specs=[pl.BlockSpec((8, D), lambda l, j: (0, 0)),
                  pl.BlockSpec((1, D, D), lambda l, j: (l, 0, j)),
                  pl.BlockSpec((1, 1, D), lambda l, j: (l, 0, j))],
        out_specs=pl.BlockSpec((1, 8, D), lambda l, j: (l, 0, j)),
        compiler_params=_params(2),
        name="adaln_mods",
    )(cond8, mod_w, mod_b.reshape(DEPTH, 1, N_MOD * D))
    return out.reshape(DEPTH, 8, N_MOD, D)


def _ffn_kernel(*refs, row0, first, final):
    refs = list(refs)
    x = _ctx_or_lat(refs.pop(0), refs.pop(0), TM_FFN) if first else refs.pop(0)[...]
    mod_ref, g_ref, win_ref, wd_ref = refs[:4]
    act_ref = refs[-1]
    m = mod_ref[0]
    h = _rms(x, g_ref[...]) * (1.0 + m[row0 + 1:row0 + 2]) + m[row0:row0 + 1]
    hb = h.astype(BF16)
    for c in range(D_FF // FF_CHUNK):
        lo = c * FF_CHUNK
        gate = jnp.dot(hb, win_ref[:, lo:lo + FF_CHUNK], preferred_element_type=F32)
        up = jnp.dot(hb, win_ref[:, D_FF + lo:D_FF + lo + FF_CHUNK], preferred_element_type=F32)
        act_ref[:, lo:lo + FF_CHUNK] = (_silu(gate) * up).astype(BF16)
    out = jnp.dot(act_ref[...], wd_ref[...], preferred_element_type=F32)
    y = x + (0.5 * m[row0 + 2:row0 + 3]) * out
    if final:
        fg_ref, oc_ref, ol_ref = refs[4:7]
        _store_ctx_lat(oc_ref, ol_ref, _rms(y, fg_ref[...]), TM_FFN)
    else:
        refs[4][...] = y


def _ffn_call(x, mods_l, g, w_in, w_down, layer, which, final_g=None):
    first = isinstance(x, tuple)
    final = final_g is not None
    in_specs = (list(_ctx_lat_specs((TM_FFN, D), TM_FFN)) if first else [_tok(D, TM_FFN)]) + [
        _modspec(TM_FFN), _full((1, D)),
        pl.BlockSpec((None, None, D, 2 * D_FF), lambda i: (layer, which, 0, 0), pipeline_mode=pl.Buffered(1)),
        pl.BlockSpec((None, None, D_FF, D), lambda i: (layer, which, 0, 0), pipeline_mode=pl.Buffered(1))]
    args = (list(x) if first else [x]) + [mods_l, _row(g), w_in, w_down]
    if final:
        in_specs.append(_full((1, D)))
        args.append(_row(final_g))
        out_shape = (jax.ShapeDtypeStruct((N_CTX, D), F32), jax.ShapeDtypeStruct((N_LAT, D), F32))
        out_specs = _ctx_lat_specs((TM_FFN, D), TM_FFN)
    else:
        out_shape = jax.ShapeDtypeStruct((N_TOK, D), F32)
        out_specs = _tok(D, TM_FFN)
    return pl.pallas_call(
        functools.partial(_ffn_kernel, row0=6 * which, first=first, final=final),
        out_shape=out_shape,
        grid=(N_TOK // TM_FFN,),
        in_specs=in_specs,
        out_specs=out_specs,
        scratch_shapes=[pltpu.VMEM((TM_FFN, D_FF), BF16)],
        compiler_params=_params(),
        name="ffn_final" if final else "ffn",
    )(*args)


def _out_kernel(oc_ref, ol_ref, x_ref, mod_ref, w_ref, y_ref):
    out = _mm(_ctx_or_lat(oc_ref, ol_ref), w_ref[...])
    y_ref[...] = x_ref[...] + mod_ref[0][5:6] * out


def _out_call(oc, ol, x, mods_l, w):
    return pl.pallas_call(
        _out_kernel,
        out_shape=jax.ShapeDtypeStruct((N_TOK, D), F32),
        grid=(N_TOK // TM,),
        in_specs=[*_ctx_lat_specs((TM, oc.shape[1])), _tok(D), _modspec(), _full(w.shape)],
        out_specs=_tok(D),
        compiler_params=_params(),
        name="out_proj",
    )(oc, ol, x, mods_l, w)


def _gqa_proj_kernel(x_ref, mod_ref, g_ref, w_ref, qn_ref, kn_ref, cos_ref, sin_ref,
                     q_ref, kc_ref, kl_ref, vc_ref, vl_ref):
    m = mod_ref[0]
    h = _rms(x_ref[...], g_ref[...]) * (1.0 + m[4:5]) + m[3:4]
    qkv = _mm(h, w_ref[...])
    ones = _seg_ones()
    cos, sin_s = cos_ref[...], sin_ref[...]
    q = qkv[:, :D]
    k = qkv[:, D:D + GQ_KVW]
    q = q * lax.rsqrt(_seg_sum(q * q, ones) * (1.0 / HD) + NORM_EPS) * qn_ref[...]
    k = k * lax.rsqrt(_seg_sum(k * k, ones) * (1.0 / HD) + NORM_EPS) * kn_ref[...]
    q_ref[...] = _rope(q, cos, sin_s)
    _store_ctx_lat(kc_ref, kl_ref, _rope(k, cos, sin_s))
    _store_ctx_lat(vc_ref, vl_ref, qkv[:, D + GQ_KVW:])


def _gqa_proj_call(x, mods_l, g, w, qn, kn, cos_t, sin_t):
    return pl.pallas_call(
        _gqa_proj_kernel,
        out_shape=(jax.ShapeDtypeStruct((N_TOK, D), F32),) + _ctx_lat_shapes(GQ_KVW) * 2,
        grid=(N_TOK // TM,),
        in_specs=[_tok(D), _modspec(), _full((1, D)), _full(w.shape), _full((1, D)), _full((1, GQ_KVW)),
                  _ropespec(), _ropespec()],
        out_specs=(_tok(D),) + _ctx_lat_specs((TM, GQ_KVW)) * 2,
        compiler_params=_params(),
        name="gqa_proj",
    )(x, mods_l, _row(g), w, _row(jnp.tile(qn, GQ_HEADS)), _row(jnp.tile(kn, GQ_KV)), cos_t, sin_t)


def _softmax_parts(q, ks):
    ss = [_mm_nt(q, k) for k in ks]
    mx = functools.reduce(jnp.maximum, [jnp.max(s, axis=-1, keepdims=True) for s in ss])
    es = [jnp.exp(s - mx) for s in ss]
    return es, functools.reduce(jnp.add, [jnp.sum(e, axis=-1, keepdims=True) for e in es])


def _attn_specs(latent, kv_width):
    if not latent:
        tok = pl.BlockSpec((SEQ, D), lambda i: (i, 0))
        kv = pl.BlockSpec((SEQ, kv_width), lambda i: (i, 0))
        return (BATCH,), tok, [kv], tok
    per = DEC_SEQ // TM
    q = pl.BlockSpec((TM, D), lambda b, j: (N_CTX // TM + b * per + j, 0))
    cache = pl.BlockSpec((PAST, kv_width), lambda b, j: (b, 0))
    own = pl.BlockSpec((DEC_SEQ, kv_width), lambda b, j: (b, 0))
    out = pl.BlockSpec((TM, D), lambda b, j: (b * per + j, 0))
    return (DEC_BATCH, per), q, [cache, own], out


def _gqa_attn_kernel(q_ref, *refs):
    o_ref = refs[-1]
    k_refs, v_refs = refs[0:-1:2], refs[1:-1:2]
    scale = HD ** -0.5
    for kv in range(GQ_KV):
        cols = slice(kv * HD, (kv + 1) * HD)
        ks = [r[:, cols] for r in k_refs]
        vs = [r[:, cols].astype(BF16) for r in v_refs]
        for g in range(GQ_GROUP):
            h = kv * GQ_GROUP + g
            es, l = _softmax_parts(q_ref[:, h * HD:(h + 1) * HD] * scale, ks)
            pv = functools.reduce(jnp.add, [_mm(e, v) for e, v in zip(es, vs)])
            o_ref[:, h * HD:(h + 1) * HD] = pv / l


def _gqa_attn_call(q, k, v, cache_k=None, cache_v=None):
    latent = cache_k is not None
    grid, q_spec, kv_specs, o_spec = _attn_specs(latent, GQ_KVW)
    parts = [(cache_k, cache_v), (k, v)] if latent else [(k, v)]
    return pl.pallas_call(
        _gqa_attn_kernel,
        out_shape=jax.ShapeDtypeStruct((N_LAT if latent else N_CTX, D), F32),
        grid=grid,
        in_specs=[q_spec] + [s for s in kv_specs for _ in range(2)],
        out_specs=o_spec,
        compiler_params=_params(len(grid)),
        name="gqa_attn",
    )(q, *[a for part in parts for a in part])


def _conv_in_kernel(x_ref, mod_ref, g_ref, w_ref, b_ref, u_ref):
    m = mod_ref[0]
    h = _rms(x_ref[...], g_ref[...]) * (1.0 + m[4:5]) + m[3:4]
    u = _mm(h, w_ref[...]) + b_ref[...]
    u_ref[...] = u[:, :D] * jax.nn.sigmoid(u[:, D:])


def _conv_in_call(x, mods_l, g, w, b):
    return pl.pallas_call(
        _conv_in_kernel,
        out_shape=jax.ShapeDtypeStruct((N_TOK, D), F32),
        grid=(N_TOK // TM,),
        in_specs=[_tok(D), _modspec(), _full((1, D)), _full(w.shape), _full((1, 2 * D))],
        out_specs=_tok(D),
        compiler_params=_params(),
        name="conv_in",
    )(x, mods_l, _row(g), w, _row(b))


HALO = 16


def _seq_edges(i):
    nt_ctx = N_CTX // TM
    per = jnp.where(i < nt_ctx, SEQ // TM, DEC_SEQ // TM)
    j = jnp.where(i < nt_ctx, i, i - nt_ctx) % per
    return j == 0, j == per - 1


def _conv_out_kernel(u_ref, up_ref, un_ref, x_ref, mod_ref, wdw_ref, bdw_ref, lg_ref, lb_ref, wo_ref, bo_ref,
                     y_ref, pad_ref):
    start, end = _seq_edges(pl.program_id(0))
    pad_ref[0:HALO, :] = jnp.where(start, 0.0, up_ref[...])
    pad_ref[HALO:HALO + TM, :] = u_ref[...]
    pad_ref[HALO + TM:, :] = jnp.where(end, 0.0, un_ref[...])
    wdw = wdw_ref[...]
    pad = pad_ref[...]
    n_pad = TM + 2 * HALO
    acc = jnp.zeros((TM, D), F32)
    for phase in range(8):
        shifted = pad if phase == 0 else pltpu.roll(pad, n_pad - phase, 0)
        for j in range(CV_WIDTH):
            off = HALO - CV_PAD + j
            if off % 8 == phase:
                acc = acc + wdw[j:j + 1] * shifted[off - phase:off - phase + TM, :]
    u = acc + bdw_ref[...]
    mu = jnp.mean(u, axis=-1, keepdims=True)
    var = jnp.mean(jnp.square(u - mu), axis=-1, keepdims=True)
    u = _silu((u - mu) * lax.rsqrt(var + LN_EPS) * lg_ref[...] + lb_ref[...])
    out = _mm(u, wo_ref[...]) + bo_ref[...]
    y_ref[...] = x_ref[...] + mod_ref[0][5:6] * out


def _conv_out_call(u, x, mods_l, wdw, bdw, lg, lb, wo, bo):
    per = TM // HALO
    last = N_TOK // HALO - 1
    return pl.pallas_call(
        _conv_out_kernel,
        out_shape=jax.ShapeDtypeStruct((N_TOK, D), F32),
        grid=(N_TOK // TM,),
        in_specs=[_tok(D),
                  pl.BlockSpec((HALO, D), lambda i: (jnp.maximum(i * per - 1, 0), 0)),
                  pl.BlockSpec((HALO, D), lambda i: (jnp.minimum((i + 1) * per, last), 0)),
                  _tok(D), _modspec(), _full((CV_WIDTH, D)), _full((1, D)), _full((1, D)), _full((1, D)),
                  _full((D, D)), _full((1, D))],
        out_specs=_tok(D),
        scratch_shapes=[pltpu.VMEM((TM + 2 * HALO, D), F32)],
        compiler_params=_params(),
        name="conv_out",
    )(u, u, u, x, mods_l, wdw, _row(bdw), _row(lg), _row(lb), wo, _row(bo))


def _diff_proj_kernel(x_ref, mod_ref, g_ref, w_ref, cos_ref, sin_ref, q_ref, kc_ref, kl_ref, vc_ref, vl_ref):
    m = mod_ref[0]
    h = _rms(x_ref[...], g_ref[...]) * (1.0 + m[4:5]) + m[3:4]
    qkv = _mm(h, w_ref[...])
    cos, sin_s = cos_ref[...], sin_ref[...]
    q_ref[...] = _rope(qkv[:, :D], cos, sin_s)
    _store_ctx_lat(kc_ref, kl_ref, _rope(qkv[:, D:2 * D], cos, sin_s))
    _store_ctx_lat(vc_ref, vl_ref, qkv[:, 2 * D:])


def _diff_proj_call(x, mods_l, g, w, cos_t, sin_t):
    return pl.pallas_call(
        _diff_proj_kernel,
        out_shape=(jax.ShapeDtypeStruct((N_TOK, D), F32),) + _ctx_lat_shapes(D) * 2,
        grid=(N_TOK // TM,),
        in_specs=[_tok(D), _modspec(), _full((1, D)), _full(w.shape), _ropespec(), _ropespec()],
        out_specs=(_tok(D),) + _ctx_lat_specs((TM, D)) * 2,
        compiler_params=_params(),
        name="diff_proj",
    )(x, mods_l, _row(g), w, cos_t, sin_t)


def _diff_attn_kernel(q_ref, lam_ref, sg_ref, *refs):
    o_ref = refs[-1]
    k_refs, v_refs = refs[0:-1:2], refs[1:-1:2]
    lv = lam_ref[...]
    lam = (jnp.exp(jnp.sum(lv[0:1] * lv[1:2], axis=-1, keepdims=True))
           - jnp.exp(jnp.sum(lv[2:3] * lv[3:4], axis=-1, keepdims=True)) + DF_LAMBDA_INIT)
    scale = HD ** -0.5
    for h in range(DF_HEADS):
        c1, c2 = (2 * h) * HD, (2 * h + 1) * HD
        es1, l1 = _softmax_parts(q_ref[:, c1:c1 + HD] * scale, [r[:, c1:c1 + HD] for r in k_refs])
        es2, l2 = _softmax_parts(q_ref[:, c2:c2 + HD] * scale, [r[:, c2:c2 + HD] for r in k_refs])
        inv1, inv2 = 1.0 / l1, lam / l2
        o = functools.reduce(jnp.add, [_mm(e1 * inv1 - e2 * inv2, r[:, c1:c1 + 2 * HD])
                                       for e1, e2, r in zip(es1, es2, v_refs)])
        o = _rms(o, sg_ref[...], DF_SUBLN_EPS) * (1.0 - DF_LAMBDA_INIT)
        o_ref[:, c1:c1 + 2 * HD] = o


def _diff_attn_call(q, k, v, lam_rows, subln_g, cache_k=None, cache_v=None):
    latent = cache_k is not None
    grid, q_spec, kv_specs, o_spec = _attn_specs(latent, D)
    parts = [(cache_k, cache_v), (k, v)] if latent else [(k, v)]
    nd = len(grid)
    return pl.pallas_call(
        _diff_attn_kernel,
        out_shape=jax.ShapeDtypeStruct((N_LAT if latent else N_CTX, D), F32),
        grid=grid,
        in_specs=[q_spec, pl.BlockSpec((4, HD), lambda *_: (0, 0)), pl.BlockSpec((1, 2 * HD), lambda *_: (0, 0))]
        + [s for s in kv_specs for _ in range(2)],
        out_specs=o_spec,
        compiler_params=_params(nd),
        name="diff_attn",
    )(q, lam_rows, _row(subln_g), *[a for part in parts for a in part])


def _rwkv_in_kernel(x_ref, xp_ref, xn_ref, mod_ref, g_ref, mix_ref, wr_ref, wk_ref, wv_ref, g1_ref, g2_ref,
                    w1_ref, w2_ref, w0_ref, a1_ref, a2_ref, a0_ref, kk_ref, ka_ref, rk_ref,
                    r_out, v_out, kn_out, g_out, bonus_out, lw_out, kd_out, al_out):
    m = mod_ref[0]
    gn = g_ref[...]

    def norm_mod(z):
        return _rms(z, gn) * (1.0 + m[4:5]) + m[3:4]

    start, end = _seq_edges(pl.program_id(0))
    h = norm_mod(x_ref[...])
    h_prev = jnp.where(start, 0.0, norm_mod(xp_ref[...])[7:8])
    h_next = jnp.where(end, 0.0, norm_mod(xn_ref[...])[0:1])
    row = lax.broadcasted_iota(jnp.int32, (TM, D), 0)
    h_dn = jnp.where(row == 0, h_prev, pltpu.roll(h, 1, 0))
    h_up = jnp.where(row == TM - 1, h_next, pltpu.roll(h, TM - 1, 0))
    xx = 0.5 * (h_dn + h_up) - h
    mix = mix_ref[...]
    xr, xw, xk, xv, xa, xg = (h + xx * mix[i:i + 1] for i in range(6))

    r = _mm(xr, wr_ref[...])
    k = _mm(xk, wk_ref[...])
    v = _mm(xv, wv_ref[...])
    g = _mm(jax.nn.sigmoid(_mm(xg, g1_ref[...])), g2_ref[...])
    wl = w0_ref[...] + _mm(jnp.tanh(_mm(xw, w1_ref[...])), w2_ref[...])
    log_decay = -jnp.exp(-0.5) * jax.nn.sigmoid(wl)
    alpha = jax.nn.sigmoid(a0_ref[...] + _mm(_mm(xa, a1_ref[...]), a2_ref[...]))

    ones = _seg_ones()
    kk = k * kk_ref[...]
    kk = kk * lax.rsqrt(jnp.maximum(_seg_sum(kk * kk, ones), 1e-24))
    r_out[...] = r
    v_out[...] = v
    kn_out[...] = kk
    g_out[...] = g
    kd_sum = jnp.zeros((TM, D), F32)
    for d in range(2):
        al = alpha[:, d * D:(d + 1) * D]
        kd = k * (1.0 + (al - 1.0) * ka_ref[...])
        kd_sum = kd_sum + kd
        lw_out[d] = log_decay[:, d * D:(d + 1) * D]
        kd_out[d] = kd
        al_out[d] = al
    bonus_out[...] = _seg_sum(r * kd_sum * rk_ref[...], ones) * v


def _rwkv_in_call(x, mods_l, g, mix, wr, wk, wv, g1, g2, w1, w2, w0, a1, a2, a0, k_k, k_a, r_k):
    per = TM // 8
    last = N_TOK // 8 - 1
    tok_out = jax.ShapeDtypeStruct((N_TOK, D), F32)
    dir_out = jax.ShapeDtypeStruct((2, N_TOK, D), F32)
    dir_spec = pl.BlockSpec((2, TM, D), lambda i: (0, i, 0))
    return pl.pallas_call(
        _rwkv_in_kernel,
        out_shape=(tok_out,) * 5 + (dir_out,) * 3,
        grid=(N_TOK // TM,),
        in_specs=[_tok(D),
                  pl.BlockSpec((8, D), lambda i: (jnp.maximum(i * per - 1, 0), 0)),
                  pl.BlockSpec((8, D), lambda i: (jnp.minimum((i + 1) * per, last), 0)),
                  _modspec(), _full((1, D)), _full((6, D)),
                  _full((D, D)), _full((D, D)), _full((D, D)), _full((D, GATE_LORA)), _full((GATE_LORA, D)),
                  _full((D, 2 * LORA)), _full((2 * LORA, 2 * D)), _full((1, 2 * D)),
                  _full((D, 2 * LORA)), _full((2 * LORA, 2 * D)), _full((1, 2 * D)),
                  _full((1, D)), _full((1, D)), _full((1, D))],
        out_specs=(_tok(D),) * 5 + (dir_spec,) * 3,
        compiler_params=_params(),
        name="rwkv_in",
    )(x, x, x, mods_l, _row(g), mix, wr, wk, wv, g1, g2, w1, w2, _row(w0), a1, a2, _row(a0),
      _row(k_k), _row(k_a), _row(r_k))


def _block_diag_rows(z):
    zb = z.astype(BF16)
    lane_head = lax.broadcasted_iota(jnp.int32, zb.shape, 1) // HD
    zero = jnp.zeros_like(zb)
    return jnp.concatenate([jnp.where(lane_head == h, zb, zero) for h in range(QUAD // HD)], axis=0)


def _wkv_prep(rev, r_ref, v_ref, kn_ref, lw_ref, kd_ref, al_ref):
    C = WKV_CHUNK
    row = lax.broadcasted_iota(jnp.int32, (C, C), 0)
    col = lax.broadcasted_iota(jnp.int32, (C, C), 1)
    incl = (col >= row) if rev else (col <= row)
    lw = lw_ref[0]
    hi, mid, lo = _split3(lw)
    inc_b = incl.astype(BF16)
    dot = functools.partial(jnp.dot, preferred_element_type=F32)
    g_incl = dot(inc_b, hi) + (dot(inc_b, mid) + dot(inc_b, lo))
    g_tot = jnp.sum(lw, axis=0, keepdims=True)
    kn = kn_ref[...]
    b = kn * al_ref[0]
    kd = kd_ref[0]
    e_neg = jnp.exp(-g_incl)
    e_rem = jnp.exp(g_tot - g_incl)
    return dict(a_t=-kn * jnp.exp(g_incl - lw), r_t=r_ref[...] * jnp.exp(g_incl), b_t=b * e_neg, k_t=kd * e_neg,
                b_h=b * e_rem, k_h=kd * e_rem, w_tot=jnp.exp(g_tot), v=v_ref[...])


def _wkv_kernel(*refs, n_chunks, has_init):
    fwd_refs, bwd_refs, rest = refs[0:6], refs[6:12], refs[12:]
    if has_init:
        s0_ref, yf_ref, yb_ref, sfin_ref, s_ref = rest
    else:
        yf_ref, yb_ref, sfin_ref, s_ref = rest
    c = pl.program_id(1)
    C = WKV_CHUNK

    @pl.when(c == 0)
    def _():
        if has_init:
            for d in range(2):
                for h in range(RW_HEADS):
                    s_ref[d, :, h * HD:(h + 1) * HD] = s0_ref[0, d, h]
        else:
            s_ref[...] = jnp.zeros_like(s_ref)

    row4 = lax.broadcasted_iota(jnp.int32, (C, QUAD), 0)
    col4 = lax.broadcasted_iota(jnp.int32, (C, QUAD), 1) % C
    r2 = lax.broadcasted_iota(jnp.int32, (QUAD, QUAD), 0)
    c2 = lax.broadcasted_iota(jnp.int32, (QUAD, QUAD), 1)
    same_head = (r2 // HD) == (c2 // HD)
    strict4 = (col4 < row4, col4 > row4)
    incl4 = (col4 <= row4, col4 >= row4)

    prep = (_wkv_prep(False, *fwd_refs), _wkv_prep(True, *bwd_refs))
    y_refs = (yf_ref, yb_ref)
    chains = [(d, slice(q * QUAD, (q + 1) * QUAD)) for d in range(2) for q in range(D // QUAD)]

    def rows(name):
        return [prep[d][name][:, sl] for d, sl in chains]

    a_t, r_t, b_t, k_t, b_h, k_h, w_tot, v = (rows(n) for n in ("a_t", "r_t", "b_t", "k_t", "b_h", "k_h", "w_tot", "v"))
    s_in = [s_ref[d, :, sl] for d, sl in chains]
    bd = _block_diag_rows
    dot = functools.partial(jnp.dot, preferred_element_type=F32)
    cat = functools.partial(jnp.concatenate, axis=0)
    ar = [cat([a, r]) for a, r in zip(a_t, r_t)]
    gram_b = [_mm_nt(z, bd(b)) for z, b in zip(ar, b_t)]
    gram_k = [_mm_nt(z, bd(k)) for z, k in zip(ar, k_t)]
    a_ab = [jnp.where(strict4[d], g[:C], 0.0) for (d, _), g in zip(chains, gram_b)]
    a_ak = [jnp.where(strict4[d], g[:C], 0.0) for (d, _), g in zip(chains, gram_k)]
    a_rb = [jnp.where(incl4[d], g[C:], 0.0) for (d, _), g in zip(chains, gram_b)]
    a_rk = [jnp.where(incl4[d], g[C:], 0.0) for (d, _), g in zip(chains, gram_k)]
    H = C // 2
    diag_blk = (row4 // H) == (col4 // H)
    l_d = [jnp.where(diag_blk, l, 0.0) for l in a_ab]
    l_o = [jnp.where(diag_blk, 0.0, l) for l in a_ab]
    rowh = lax.broadcasted_iota(jnp.int32, (H, QUAD), 0)
    laneh = lax.broadcasted_iota(jnp.int32, (H, QUAD), 1)
    eye8 = ((laneh % H) == rowh).astype(F32)

    def bd8(z):
        grp = lax.broadcasted_iota(jnp.int32, z.shape, 1) // H
        zero = jnp.zeros_like(z)
        return cat([jnp.where(grp == g, z, zero) for g in range(QUAD // H)])

    p = [z[:H] + z[H:] for z in l_d]
    x = [eye8 + z for z in p]
    n_steps = 5
    for i in range(n_steps):
        ph, plo = zip(*[_split2(pq) for pq in p])
        wh = [bd8(z) for z in ph]
        wl = [bd8(z) for z in plo]
        if i == 0:
            r1 = [dot(cat([h, l]), w) for h, l, w in zip(ph, plo, wh)]
            r2 = [dot(h, w) for h, w in zip(ph, wl)]
            p = [a[:H] + (b_ + a[H:]) for a, b_ in zip(r1, r2)]
            continue
        xh, xlo = zip(*[_split2(xq) for xq in x])
        if i < n_steps - 1:
            r1 = [dot(cat([h, l, g, m]), w) for h, l, g, m, w in zip(ph, plo, xh, xlo, wh)]
            r2 = [dot(cat([h, g]), w) for h, g, w in zip(ph, xh, wl)]
            p = [a[:H] + (b_[:H] + a[H:2 * H]) for a, b_ in zip(r1, r2)]
            x = [xq + (a[2 * H:3 * H] + (b_[H:] + a[3 * H:])) for xq, a, b_ in zip(x, r1, r2)]
        else:
            r1 = [dot(cat([g, m]), w) for g, m, w in zip(xh, xlo, wh)]
            r2 = [dot(g, w) for g, w in zip(xh, wl)]
            x = [xq + (a[:H] + (b_ + a[H:])) for xq, a, b_ in zip(x, r1, r2)]
    top = (laneh % C) < H
    t_d = [cat([jnp.where(top, xq, 0.0), jnp.where(top, 0.0, xq)]) for xq in x]

    def times_bd(lhs, w):
        lh, ll = _split2(lhs)
        wh_, wl_ = _split2(w)
        r1_ = dot(cat([lh, ll]), bd(wh_))
        return r1_[:C] + (dot(lh, bd(wl_)) + r1_[C:])

    z = [times_bd(t, lo_) for t, lo_ in zip(t_d, l_o)]
    x = [t + times_bd(zq, t) for t, zq in zip(t_d, z)]
    vbd = [bd(z) for z in v]
    rv = [_mm(cat([m, n]), vb) for m, n, vb in zip(a_ak, a_rk, vbd)]
    akv = [z[:C] for z in rv]
    wm = [_mm(xq, bd(a)) for xq, a in zip(x, a_t)]
    um = [_mm(xq, bd(z)) for xq, z in zip(x, akv)]
    qm = [r + _mm(m, bd(w)) for r, m, w in zip(r_t, a_rb, wm)]
    y0 = [_mm(m, bd(z)) + z2[C:] for m, z, z2 in zip(a_rb, um, rv)]
    rs = [_mm_nt(cat([w, qq]), bd(sq)) for w, qq, sq in zip(wm, qm, s_in)]
    u = [z[:C] + z2 for z, z2 in zip(rs, um)]
    for (d, sl), z, yy in zip(chains, rs, y0):
        y_refs[d][:, sl] = z[C:] + yy
    for (d, sl), uq, vq, bq, kq, wq, sq in zip(chains, u, v, b_h, k_h, w_tot, s_in):
        upd = _mm_tn(jnp.concatenate([uq, vq], axis=0), jnp.concatenate([bq, kq], axis=0))
        upd = jnp.where(same_head, upd, 0.0)
        upd = (upd[0:HD] + upd[HD:2 * HD]) + (upd[2 * HD:3 * HD] + upd[3 * HD:])
        s_ref[d, :, sl] = sq * wq + upd

    @pl.when(c == n_chunks - 1)
    def _():
        for d in range(2):
            for h in range(RW_HEADS):
                sfin_ref[0, d, h] = s_ref[d, :, h * HD:(h + 1) * HD]


def _wkv_call(r, v, kn, lw, kd, al, n_seq, seq_len, row_off, s0=None):
    C = WKV_CHUNK
    n_chunks = seq_len // C
    off = row_off // C

    def fwd(s, c):
        return s * n_chunks + c

    def bwd(s, c):
        return s * n_chunks + n_chunks - 1 - c

    def specs(blk, d):
        tok = pl.BlockSpec((C, D), lambda s, c: (off + blk(s, c), 0))
        dirtok = pl.BlockSpec((1, C, D), lambda s, c: (d, off + blk(s, c), 0))
        return [tok, tok, tok, dirtok, dirtok, dirtok]

    state = pl.BlockSpec((1, 2, RW_HEADS, HD, HD), lambda s, c: (s, 0, 0, 0, 0))
    has_init = s0 is not None
    in_specs = specs(fwd, 0) + specs(bwd, 1) + ([state] if has_init else [])
    args = [r, v, kn, lw, kd, al] * 2 + ([s0] if has_init else [])
    y_shape = jax.ShapeDtypeStruct((n_seq * seq_len, D), F32)
    return pl.pallas_call(
        functools.partial(_wkv_kernel, n_chunks=n_chunks, has_init=has_init),
        out_shape=(y_shape, y_shape, jax.ShapeDtypeStruct((n_seq, 2, RW_HEADS, HD, HD), F32)),
        grid=(n_seq, n_chunks),
        in_specs=in_specs,
        out_specs=(pl.BlockSpec((C, D), lambda s, c: (fwd(s, c), 0)),
                   pl.BlockSpec((C, D), lambda s, c: (bwd(s, c), 0)), state),
        scratch_shapes=[pltpu.VMEM((2, HD, D), F32)],
        compiler_params=_params(2),
        name="wkv_scan",
    )(*args)


def _rwkv_out_kernel(yfc_ref, yfl_ref, ybc_ref, ybl_ref, bonus_ref, g_ref, x_ref, mod_ref, lg_ref, lb_ref, wo_ref,
                     o_ref):
    ones = _seg_ones()
    y = _ctx_or_lat(yfc_ref, yfl_ref) + _ctx_or_lat(ybc_ref, ybl_ref)
    mu = _seg_sum(y, ones) * (1.0 / HD)
    yc = y - mu
    var = _seg_sum(yc * yc, ones) * (1.0 / HD)
    out = yc * lax.rsqrt(var + RW_GN_EPS) * lg_ref[...] + lb_ref[...] + bonus_ref[...]
    o_ref[...] = x_ref[...] + mod_ref[0][5:6] * _mm(out * g_ref[...], wo_ref[...])


def _rwkv_out_call(y_ctx, y_lat, bonus, g, x, mods_l, lg, lb, wo):
    return pl.pallas_call(
        _rwkv_out_kernel,
        out_shape=jax.ShapeDtypeStruct((N_TOK, D), F32),
        grid=(N_TOK // TM,),
        in_specs=[*_ctx_lat_specs((TM, D)), *_ctx_lat_specs((TM, D)), _tok(D), _tok(D), _tok(D), _modspec(),
                  _full((1, D)), _full((1, D)), _full((D, D))],
        out_specs=_tok(D),
        compiler_params=_params(),
        name="rwkv_out",
    )(y_ctx[0], y_lat[0], y_ctx[1], y_lat[1], bonus, g, x, mods_l, _row(lg), _row(lb), wo)


def _rope_tables():
    t = jnp.arange(DEC_SEQ)
    rowp = (t // GRID_W).astype(F32)
    colp = (t % GRID_W).astype(F32)
    axis_dim = HD // 2
    freqs = ROPE_THETA ** (-jnp.arange(0, axis_dim, 2, dtype=F32) / axis_dim)
    ang = jnp.concatenate([rowp[:, None] * freqs, colp[:, None] * freqs], axis=-1)
    cos = jnp.repeat(jnp.cos(ang), 2, axis=-1)
    sin = jnp.repeat(jnp.sin(ang), 2, axis=-1) * jnp.tile(jnp.array([-1.0, 1.0], F32), HD // 2)
    cos = jnp.concatenate([jnp.ones((TM, HD), F32), cos], axis=0)
    sin = jnp.concatenate([jnp.zeros((TM, HD), F32), sin], axis=0)
    return jnp.tile(cos, (1, LANES // HD)), jnp.tile(sin, (1, LANES // HD))


def _block_diag2(w):
    z = jnp.zeros_like(w[0])
    return jnp.concatenate([jnp.concatenate([w[0], z], axis=1), jnp.concatenate([z, w[1]], axis=1)], axis=0)


def kernel(x_prompt, x_sample, c, c_ctx, cache_k0, cache_v0, cache_k2, cache_v2, state_wkv3,
           norm_g, mod_w, mod_b, ffn_w_in, ffn_w_down, final_norm_g,
           gq_w_qkv, gq_q_norm, gq_k_norm, gq_w_o,
           cv_w_in, cv_b_in, cv_w_dw, cv_b_dw, cv_ln_g, cv_ln_b, cv_w_out, cv_b_out,
           df_w_qkv, df_lambda_q1, df_lambda_k1, df_lambda_q2, df_lambda_k2, df_subln_g, df_w_o,
           rw_mix, rw_w_r, rw_w_k, rw_w_v, rw_w_o, rw_k_k, rw_k_a, rw_r_k, rw_g1, rw_g2,
           rw_ln_g, rw_ln_b, rw_w0, rw_w1, rw_w2, rw_a0, rw_a1, rw_a2):
    bf = lambda w: w.astype(BF16)
    cond8 = jnp.concatenate([c_ctx[None, :], c, jnp.zeros((8 - 1 - DEC_BATCH, D), F32)], axis=0)
    mods = _mods_call(cond8, mod_w, mod_b)
    cos_t, sin_t = _rope_tables()

    w_in_all, w_down_all = bf(ffn_w_in), bf(ffn_w_down)

    def ffn(x, layer, which, final_g=None):
        return _ffn_call(x, mods[layer], norm_g[layer, 2 * which], w_in_all, w_down_all, layer, which, final_g)

    x = ffn((x_prompt.reshape(N_CTX, D), x_sample.reshape(N_LAT, D)), 0, 0)
    q, kc, kl, vc, vl = _gqa_proj_call(x, mods[0], norm_g[0, 1], bf(gq_w_qkv), gq_q_norm, gq_k_norm, cos_t, sin_t)
    new_k0 = kc.reshape(BATCH, SEQ, GQ_KV, HD)
    new_v0 = vc.reshape(BATCH, SEQ, GQ_KV, HD)
    oc = _gqa_attn_call(q, kc, vc)
    ol = _gqa_attn_call(q, kl, vl, cache_k0.reshape(DEC_BATCH * PAST, GQ_KVW),
                        cache_v0.reshape(DEC_BATCH * PAST, GQ_KVW))
    x = _out_call(oc, ol, x, mods[0], bf(gq_w_o))
    x = ffn(x, 0, 1)

    x = ffn(x, 1, 0)
    u = _conv_in_call(x, mods[1], norm_g[1, 1], bf(cv_w_in), cv_b_in)
    x = _conv_out_call(u, x, mods[1], cv_w_dw, cv_b_dw, cv_ln_g, cv_ln_b, bf(cv_w_out), cv_b_out)
    x = ffn(x, 1, 1)

    x = ffn(x, 2, 0)
    q, kc, kl, vc, vl = _diff_proj_call(x, mods[2], norm_g[2, 1], bf(df_w_qkv), cos_t, sin_t)
    new_k2 = kc.reshape(BATCH, SEQ, DF_HEADS, 2, HD)
    new_v2 = vc.reshape(BATCH, SEQ, DF_HEADS, 2 * HD)
    lam_rows = jnp.stack([df_lambda_q1, df_lambda_k1, df_lambda_q2, df_lambda_k2]).astype(F32)
    oc = _diff_attn_call(q, kc, vc, lam_rows, df_subln_g)
    ol = _diff_attn_call(q, kl, vl, lam_rows, df_subln_g, cache_k2.reshape(DEC_BATCH * PAST, D),
                         cache_v2.reshape(DEC_BATCH * PAST, D))
    x = _out_call(oc, ol, x, mods[2], bf(df_w_o))
    x = ffn(x, 2, 1)

    x = ffn(x, 3, 0)
    w1 = bf(jnp.concatenate([rw_w1[0], rw_w1[1]], axis=1))
    a1 = bf(jnp.concatenate([rw_a1[0], rw_a1[1]], axis=1))
    r, v, kn, g, bonus, lw, kd, al = _rwkv_in_call(
        x, mods[3], norm_g[3, 1], rw_mix, bf(rw_w_r), bf(rw_w_k), bf(rw_w_v), bf(rw_g1), bf(rw_g2),
        w1, bf(_block_diag2(rw_w2)), rw_w0.reshape(-1), a1, bf(_block_diag2(rw_a2)), rw_a0.reshape(-1),
        rw_k_k, rw_k_a, rw_r_k.reshape(-1))
    *y_ctx, new_wkv3 = _wkv_call(r, v, kn, lw, kd, al, BATCH, SEQ, 0)
    *y_lat, _ = _wkv_call(r, v, kn, lw, kd, al, DEC_BATCH, DEC_SEQ, N_CTX, state_wkv3)
    x = _rwkv_out_call(y_ctx, y_lat, bonus, g, x, mods[3], rw_ln_g, rw_ln_b, bf(rw_w_o))
    y_prompt, y_sample = ffn(x, 3, 1, final_norm_g)
    y_prompt = y_prompt.reshape(BATCH, SEQ, D)
    y_sample = y_sample.reshape(DEC_BATCH, DEC_SEQ, D)
    return (y_prompt, y_sample, new_k0, new_v0, new_k2, new_v2, new_wkv3)
```

```python
import functools

import jax
import jax.numpy as jnp
from jax import lax
from jax.experimental import pallas as pl
from jax.experimental.pallas import tpu as pltpu

F32 = jnp.float32
BF16 = jnp.bfloat16

D = 1024
BATCH, SEQ = 32, 256
DEC_BATCH, DEC_SEQ = 2, 2048
PAST = 512
DEPTH = 4
GRID_W = 64
N_MOD = 9
NORM_EPS = 1e-6
LN_EPS = 1e-5
ROPE_THETA = 10000.0
D_FF = 2816
HD = 64
GQ_HEADS, GQ_KV = 16, 4
GQ_GROUP = GQ_HEADS // GQ_KV
GQ_KVW = GQ_KV * HD
CV_WIDTH = 31
CV_PAD = CV_WIDTH // 2
DF_HEADS = 8
DF_LAMBDA_INIT = 0.470713018
DF_SUBLN_EPS = 1e-5
RW_HEADS = 16
RW_GN_EPS = 64e-5
LORA = 64
GATE_LORA = 128

N_CTX = BATCH * SEQ
N_LAT = DEC_BATCH * DEC_SEQ
N_TOK = N_CTX + N_LAT

LANES = 128
TM = 256
TM_FFN = 512
FF_CHUNK = 256
WKV_CHUNK = 64
QUAD = 4 * HD
VMEM_LIMIT = 56 * 1024 * 1024


def _mm(a, b):
    return jnp.dot(a.astype(BF16), b.astype(BF16), preferred_element_type=F32)


def _mm_nt(a, b):
    return lax.dot_general(a.astype(BF16), b.astype(BF16), (((1,), (1,)), ((), ())),
                           preferred_element_type=F32)


def _mm_tn(a, b):
    return lax.dot_general(a.astype(BF16), b.astype(BF16), (((0,), (0,)), ((), ())),
                           preferred_element_type=F32)


def _split3(x):
    hi = x.astype(BF16)
    r1 = x - hi.astype(F32)
    mid = r1.astype(BF16)
    lo = (r1 - mid.astype(F32)).astype(BF16)
    return hi, mid, lo


def _split2(x):
    hi = x.astype(BF16)
    return hi, (x - hi.astype(F32)).astype(BF16)


def _silu(x):
    return x * jax.nn.sigmoid(x)


def _rms(x, g, eps=NORM_EPS):
    return x * lax.rsqrt(jnp.mean(x * x, axis=-1, keepdims=True) + eps) * g


def _seg_ones():
    r = lax.broadcasted_iota(jnp.int32, (LANES, LANES), 0) // HD
    c = lax.broadcasted_iota(jnp.int32, (LANES, LANES), 1) // HD
    return (r == c).astype(BF16)


def _seg_sum(x, ones):
    outs = []
    for g in range(x.shape[1] // LANES):
        hi, lo = _split2(x[:, g * LANES:(g + 1) * LANES])
        dot = functools.partial(jnp.dot, preferred_element_type=F32)
        outs.append(dot(hi, ones) + dot(lo, ones))
    return outs[0] if len(outs) == 1 else jnp.concatenate(outs, axis=1)


def _rope(x, cos, sin_s):
    lane = lax.broadcasted_iota(jnp.int32, (x.shape[0], LANES), 1)
    even = (lane % 2) == 0
    outs = []
    for g in range(x.shape[1] // LANES):
        xg = x[:, g * LANES:(g + 1) * LANES]
        partner = jnp.where(even, pltpu.roll(xg, LANES - 1, 1), pltpu.roll(xg, 1, 1))
        outs.append(xg * cos + partner * sin_s)
    return outs[0] if len(outs) == 1 else jnp.concatenate(outs, axis=1)


def _cond_row(i, tm):
    nt_ctx = N_CTX // tm
    return jnp.where(i < nt_ctx, 0, 1 + (i - nt_ctx) // (DEC_SEQ // tm))


def _tok(width, tm=TM):
    return pl.BlockSpec((tm, width), lambda i: (i, 0))


def _full(shape):
    nd = len(shape)
    return pl.BlockSpec(tuple(shape), lambda i: (0,) * nd, pipeline_mode=pl.Buffered(1))


def _modspec(tm=TM):
    return pl.BlockSpec((1, N_MOD, D), lambda i: (_cond_row(i, tm), 0, 0))


def _ropespec():
    nt_ctx = N_CTX // TM
    per_seq = DEC_SEQ // TM
    return pl.BlockSpec((TM, LANES), lambda i: (jnp.where(i < nt_ctx, 0, 1 + (i - nt_ctx) % per_seq), 0))


def _params(n_axes=1):
    return pltpu.CompilerParams(dimension_semantics=("arbitrary",) * n_axes, vmem_limit_bytes=VMEM_LIMIT)


def _row(v):
    return v.reshape(1, -1).astype(F32)


def _ctx_lat_specs(block, tm=TM):
    nt_ctx = N_CTX // tm
    lead = (0,) * (len(block) - 2)
    return (pl.BlockSpec(block, lambda i: lead + (jnp.minimum(i, nt_ctx - 1), 0)),
            pl.BlockSpec(block, lambda i: lead + (jnp.maximum(i - nt_ctx, 0), 0)))


def _ctx_lat_shapes(width):
    return (jax.ShapeDtypeStruct((N_CTX, width), F32), jax.ShapeDtypeStruct((N_LAT, width), F32))


def _ctx_or_lat(c_ref, l_ref, tm=TM):
    return jnp.where(pl.program_id(0) < N_CTX // tm, c_ref[...], l_ref[...])


def _store_ctx_lat(c_ref, l_ref, val, tm=TM):
    l_ref[...] = val

    @pl.when(pl.program_id(0) < N_CTX // tm)
    def _():
        c_ref[...] = val


def _mods_kernel(c_ref, w_ref, b_ref, o_ref):
    o_ref[0] = _mm(_silu(c_ref[...]), w_ref[0]) + b_ref[0]


def _mods_call(cond8, mod_w, mod_b):
    out = pl.pallas_call(
        _mods_kernel,
        out_shape=jax.ShapeDtypeStruct((DEPTH, 8, N_MOD * D), F32),
        grid=(DEPTH, N_MOD),
        in_specs=[pl.BlockSpec((8, D), lambda l, j: (0, 0)),
                  pl.BlockSpec((1, D, D), lambda l, j: (l, 0, j)),
                  pl.BlockSpec((1, 1, D), lambda l, j: (l, 0, j))],
        out_specs=pl.BlockSpec((1, 8, D), lambda l, j: (l, 0, j)),
        compiler_params=_params(2),
        name="adaln_mods",
    )(cond8, mod_w, mod_b.reshape(DEPTH, 1, N_MOD * D))
    return out.reshape(DEPTH, 8, N_MOD, D)


def _ffn_kernel(*refs, row0, first, final):
    refs = list(refs)
    x = _ctx_or_lat(refs.pop(0), refs.pop(0), TM_FFN) if first else refs.pop(0)[...]
    mod_ref, g_ref, win_ref, wd_ref = refs[:4]
    act_ref = refs[-1]
    m = mod_ref[0]
    h = _rms(x, g_ref[...]) * (1.0 + m[row0 + 1:row0 + 2]) + m[row0:row0 + 1]
    hb = h.astype(BF16)
    for c in range(D_FF // FF_CHUNK):
        lo = c * FF_CHUNK
        gate = jnp.dot(hb, win_ref[:, lo:lo + FF_CHUNK], preferred_element_type=F32)
        up = jnp.dot(hb, win_ref[:, D_FF + lo:D_FF + lo + FF_CHUNK], preferred_element_type=F32)
        act_ref[:, lo:lo + FF_CHUNK] = (_silu(gate) * up).astype(BF16)
    out = jnp.dot(act_ref[...], wd_ref[...], preferred_element_type=F32)
    y = x + (0.5 * m[row0 + 2:row0 + 3]) * out
    if final:
        fg_ref, oc_ref, ol_ref = refs[4:7]
        _store_ctx_lat(oc_ref, ol_ref, _rms(y, fg_ref[...]), TM_FFN)
    else:
        refs[4][...] = y


def _ffn_call(x, mods_l, g, w_in, w_down, layer, which, final_g=None):
    first = isinstance(x, tuple)
    final = final_g is not None
    in_specs = (list(_ctx_lat_specs((TM_FFN, D), TM_FFN)) if first else [_tok(D, TM_FFN)]) + [
        _modspec(TM_FFN), _full((1, D)),
        pl.BlockSpec((None, None, D, 2 * D_FF), lambda i: (layer, which, 0, 0), pipeline_mode=pl.Buffered(1)),
        pl.BlockSpec((None, None, D_FF, D), lambda i: (layer, which, 0, 0), pipeline_mode=pl.Buffered(1))]
    args = (list(x) if first else [x]) + [mods_l, _row(g), w_in, w_down]
    if final:
        in_specs.append(_full((1, D)))
        args.append(_row(final_g))
        out_shape = (jax.ShapeDtypeStruct((N_CTX, D), F32), jax.ShapeDtypeStruct((N_LAT, D), F32))
        out_specs = _ctx_lat_specs((TM_FFN, D), TM_FFN)
    else:
        out_shape = jax.ShapeDtypeStruct((N_TOK, D), F32)
        out_specs = _tok(D, TM_FFN)
    return pl.pallas_call(
        functools.partial(_ffn_kernel, row0=6 * which, first=first, final=final),
        out_shape=out_shape,
        grid=(N_TOK // TM_FFN,),
        in_specs=in_specs,
        out_specs=out_specs,
        scratch_shapes=[pltpu.VMEM((TM_FFN, D_FF), BF16)],
        compiler_params=_params(),
        name="ffn_final" if final else "ffn",
    )(*args)


def _out_kernel(oc_ref, ol_ref, x_ref, mod_ref, w_ref, y_ref):
    out = _mm(_ctx_or_lat(oc_ref, ol_ref), w_ref[...])
    y_ref[...] = x_ref[...] + mod_ref[0][5:6] * out


def _out_call(oc, ol, x, mods_l, w):
    return pl.pallas_call(
        _out_kernel,
        out_shape=jax.ShapeDtypeStruct((N_TOK, D), F32),
        grid=(N_TOK // TM,),
        in_specs=[*_ctx_lat_specs((TM, oc.shape[1])), _tok(D), _modspec(), _full(w.shape)],
        out_specs=_tok(D),
        compiler_params=_params(),
        name="out_proj",
    )(oc, ol, x, mods_l, w)


def _gqa_proj_kernel(x_ref, mod_ref, g_ref, w_ref, qn_ref, kn_ref, cos_ref, sin_ref,
                     q_ref, kc_ref, kl_ref, vc_ref, vl_ref):
    m = mod_ref[0]
    h = _rms(x_ref[...], g_ref[...]) * (1.0 + m[4:5]) + m[3:4]
    qkv = _mm(h, w_ref[...])
    ones = _seg_ones()
    cos, sin_s = cos_ref[...], sin_ref[...]
    q = qkv[:, :D]
    k = qkv[:, D:D + GQ_KVW]
    q = q * lax.rsqrt(_seg_sum(q * q, ones) * (1.0 / HD) + NORM_EPS) * qn_ref[...]
    k = k * lax.rsqrt(_seg_sum(k * k, ones) * (1.0 / HD) + NORM_EPS) * kn_ref[...]
    q_ref[...] = _rope(q, cos, sin_s)
    _store_ctx_lat(kc_ref, kl_ref, _rope(k, cos, sin_s))
    _store_ctx_lat(vc_ref, vl_ref, qkv[:, D + GQ_KVW:])


def _gqa_proj_call(x, mods_l, g, w, qn, kn, cos_t, sin_t):
    return pl.pallas_call(
        _gqa_proj_kernel,
        out_shape=(jax.ShapeDtypeStruct((N_TOK, D), F32),) + _ctx_lat_shapes(GQ_KVW) * 2,
        grid=(N_TOK // TM,),
        in_specs=[_tok(D), _modspec(), _full((1, D)), _full(w.shape), _full((1, D)), _full((1, GQ_KVW)),
                  _ropespec(), _ropespec()],
        out_specs=(_tok(D),) + _ctx_lat_specs((TM, GQ_KVW)) * 2,
        compiler_params=_params(),
        name="gqa_proj",
    )(x, mods_l, _row(g), w, _row(jnp.tile(qn, GQ_HEADS)), _row(jnp.tile(kn, GQ_KV)), cos_t, sin_t)


def _softmax_parts(q, ks):
    ss = [_mm_nt(q, k) for k in ks]
    mx = functools.reduce(jnp.maximum, [jnp.max(s, axis=-1, keepdims=True) for s in ss])
    es = [jnp.exp(s - mx) for s in ss]
    return es, functools.reduce(jnp.add, [jnp.sum(e, axis=-1, keepdims=True) for e in es])


def _attn_specs(latent, kv_width):
    if not latent:
        tok = pl.BlockSpec((SEQ, D), lambda i: (i, 0))
        kv = pl.BlockSpec((SEQ, kv_width), lambda i: (i, 0))
        return (BATCH,), tok, [kv], tok
    per = DEC_SEQ // TM
    q = pl.BlockSpec((TM, D), lambda b, j: (N_CTX // TM + b * per + j, 0))
    cache = pl.BlockSpec((PAST, kv_width), lambda b, j: (b, 0))
    own = pl.BlockSpec((DEC_SEQ, kv_width), lambda b, j: (b, 0))
    out = pl.BlockSpec((TM, D), lambda b, j: (b * per + j, 0))
    return (DEC_BATCH, per), q, [cache, own], out


def _gqa_attn_kernel(q_ref, *refs):
    o_ref = refs[-1]
    k_refs, v_refs = refs[0:-1:2], refs[1:-1:2]
    scale = HD ** -0.5
    for kv in range(GQ_KV):
        cols = slice(kv * HD, (kv + 1) * HD)
        ks = [r[:, cols] for r in k_refs]
        vs = [r[:, cols].astype(BF16) for r in v_refs]
        for g in range(GQ_GROUP):
            h = kv * GQ_GROUP + g
            es, l = _softmax_parts(q_ref[:, h * HD:(h + 1) * HD] * scale, ks)
            pv = functools.reduce(jnp.add, [_mm(e, v) for e, v in zip(es, vs)])
            o_ref[:, h * HD:(h + 1) * HD] = pv / l


def _gqa_attn_call(q, k, v, cache_k=None, cache_v=None):
    latent = cache_k is not None
    grid, q_spec, kv_specs, o_spec = _attn_specs(latent, GQ_KVW)
    parts = [(cache_k, cache_v), (k, v)] if latent else [(k, v)]
    return pl.pallas_call(
        _gqa_attn_kernel,
        out_shape=jax.ShapeDtypeStruct((N_LAT if latent else N_CTX, D), F32),
        grid=grid,
        in_specs=[q_spec] + [s for s in kv_specs for _ in range(2)],
        out_specs=o_spec,
        compiler_params=_params(len(grid)),
        name="gqa_attn",
    )(q, *[a for part in parts for a in part])


def _conv_in_kernel(x_ref, mod_ref, g_ref, w_ref, b_ref, u_ref):
    m = mod_ref[0]
    h = _rms(x_ref[...], g_ref[...]) * (1.0 + m[4:5]) + m[3:4]
    u = _mm(h, w_ref[...]) + b_ref[...]
    u_ref[...] = u[:, :D] * jax.nn.sigmoid(u[:, D:])


def _conv_in_call(x, mods_l, g, w, b):
    return pl.pallas_call(
        _conv_in_kernel,
        out_shape=jax.ShapeDtypeStruct((N_TOK, D), F32),
        grid=(N_TOK // TM,),
        in_specs=[_tok(D), _modspec(), _full((1, D)), _full(w.shape), _full((1, 2 * D))],
        out_specs=_tok(D),
        compiler_params=_params(),
        name="conv_in",
    )(x, mods_l, _row(g), w, _row(b))


HALO = 16


def _seq_edges(i):
    nt_ctx = N_CTX // TM
    per = jnp.where(i < nt_ctx, SEQ // TM, DEC_SEQ // TM)
    j = jnp.where(i < nt_ctx, i, i - nt_ctx) % per
    return j == 0, j == per - 1


def _conv_out_kernel(u_ref, up_ref, un_ref, x_ref, mod_ref, wdw_ref, bdw_ref, lg_ref, lb_ref, wo_ref, bo_ref,
                     y_ref, pad_ref):
    start, end = _seq_edges(pl.program_id(0))
    pad_ref[0:HALO, :] = jnp.where(start, 0.0, up_ref[...])
    pad_ref[HALO:HALO + TM, :] = u_ref[...]
    pad_ref[HALO + TM:, :] = jnp.where(end, 0.0, un_ref[...])
    wdw = wdw_ref[...]
    pad = pad_ref[...]
    n_pad = TM + 2 * HALO
    acc = jnp.zeros((TM, D), F32)
    for phase in range(8):
        shifted = pad if phase == 0 else pltpu.roll(pad, n_pad - phase, 0)
        for j in range(CV_WIDTH):
            off = HALO - CV_PAD + j
            if off % 8 == phase:
                acc = acc + wdw[j:j + 1] * shifted[off - phase:off - phase + TM, :]
    u = acc + bdw_ref[...]
    mu = jnp.mean(u, axis=-1, keepdims=True)
    var = jnp.mean(jnp.square(u - mu), axis=-1, keepdims=True)
    u = _silu((u - mu) * lax.rsqrt(var + LN_EPS) * lg_ref[...] + lb_ref[...])
    out = _mm(u, wo_ref[...]) + bo_ref[...]
    y_ref[...] = x_ref[...] + mod_ref[0][5:6] * out


def _conv_out_call(u, x, mods_l, wdw, bdw, lg, lb, wo, bo):
    per = TM // HALO
    last = N_TOK // HALO - 1
    return pl.pallas_call(
        _conv_out_kernel,
        out_shape=jax.ShapeDtypeStruct((N_TOK, D), F32),
        grid=(N_TOK // TM,),
        in_specs=[_tok(D),
                  pl.BlockSpec((HALO, D), lambda i: (jnp.maximum(i * per - 1, 0), 0)),
                  pl.BlockSpec((HALO, D), lambda i: (jnp.minimum((i + 1) * per, last), 0)),
                  _tok(D), _modspec(), _full((CV_WIDTH, D)), _full((1, D)), _full((1, D)), _full((1, D)),
                  _full((D, D)), _full((1, D))],
        out_specs=_tok(D),
        scratch_shapes=[pltpu.VMEM((TM + 2 * HALO, D), F32)],
        compiler_params=_params(),
        name="conv_out",
    )(u, u, u, x, mods_l, wdw, _row(bdw), _row(lg), _row(lb), wo, _row(bo))


def _diff_proj_kernel(x_ref, mod_ref, g_ref, w_ref, cos_ref, sin_ref, q_ref, kc_ref, kl_ref, vc_ref, vl_ref):
    m = mod_ref[0]
    h = _rms(x_ref[...], g_ref[...]) * (1.0 + m[4:5]) + m[3:4]
    qkv = _mm(h, w_ref[...])
    cos, sin_s = cos_ref[...], sin_ref[...]
    q_ref[...] = _rope(qkv[:, :D], cos, sin_s)
    _store_ctx_lat(kc_ref, kl_ref, _rope(qkv[:, D:2 * D], cos, sin_s))
    _store_ctx_lat(vc_ref, vl_ref, qkv[:, 2 * D:])


def _diff_proj_call(x, mods_l, g, w, cos_t, sin_t):
    return pl.pallas_call(
        _diff_proj_kernel,
        out_shape=(jax.ShapeDtypeStruct((N_TOK, D), F32),) + _ctx_lat_shapes(D) * 2,
        grid=(N_TOK // TM,),
        in_specs=[_tok(D), _modspec(), _full((1, D)), _full(w.shape), _ropespec(), _ropespec()],
        out_specs=(_tok(D),) + _ctx_lat_specs((TM, D)) * 2,
        compiler_params=_params(),
        name="diff_proj",
    )(x, mods_l, _row(g), w, cos_t, sin_t)


def _diff_attn_kernel(q_ref, lam_ref, sg_ref, *refs):
    o_ref = refs[-1]
    k_refs, v_refs = refs[0:-1:2], refs[1:-1:2]
    lv = lam_ref[...]
    lam = (jnp.exp(jnp.sum(lv[0:1] * lv[1:2], axis=-1, keepdims=True))
           - jnp.exp(jnp.sum(lv[2:3] * lv[3:4], axis=-1, keepdims=True)) + DF_LAMBDA_INIT)
    scale = HD ** -0.5
    for h in range(DF_HEADS):
        c1, c2 = (2 * h) * HD, (2 * h + 1) * HD
        es1, l1 = _softmax_parts(q_ref[:, c1:c1 + HD] * scale, [r[:, c1:c1 + HD] for r in k_refs])
        es2, l2 = _softmax_parts(q_ref[:, c2:c2 + HD] * scale, [r[:, c2:c2 + HD] for r in k_refs])
        inv1, inv2 = 1.0 / l1, lam / l2
        o = functools.reduce(jnp.add, [_mm(e1 * inv1 - e2 * inv2, r[:, c1:c1 + 2 * HD])
                                       for e1, e2, r in zip(es1, es2, v_refs)])
        o = _rms(o, sg_ref[...], DF_SUBLN_EPS) * (1.0 - DF_LAMBDA_INIT)
        o_ref[:, c1:c1 + 2 * HD] = o


def _diff_attn_call(q, k, v, lam_rows, subln_g, cache_k=None, cache_v=None):
    latent = cache_k is not None
    grid, q_spec, kv_specs, o_spec = _attn_specs(latent, D)
    parts = [(cache_k, cache_v), (k, v)] if latent else [(k, v)]
    nd = len(grid)
    return pl.pallas_call(
        _diff_attn_kernel,
        out_shape=jax.ShapeDtypeStruct((N_LAT if latent else N_CTX, D), F32),
        grid=grid,
        in_specs=[q_spec, pl.BlockSpec((4, HD), lambda *_: (0, 0)), pl.BlockSpec((1, 2 * HD), lambda *_: (0, 0))]
        + [s for s in kv_specs for _ in range(2)],
        out_specs=o_spec,
        compiler_params=_params(nd),
        name="diff_attn",
    )(q, lam_rows, _row(subln_g), *[a for part in parts for a in part])


def _rwkv_in_kernel(x_ref, xp_ref, xn_ref, mod_ref, g_ref, mix_ref, wr_ref, wk_ref, wv_ref, g1_ref, g2_ref,
                    w1_ref, w2_ref, w0_ref, a1_ref, a2_ref, a0_ref, kk_ref, ka_ref, rk_ref,
                    r_out, v_out, kn_out, g_out, bonus_out, lw_out, kd_out, al_out):
    m = mod_ref[0]
    gn = g_ref[...]

    def norm_mod(z):
        return _rms(z, gn) * (1.0 + m[4:5]) + m[3:4]

    start, end = _seq_edges(pl.program_id(0))
    h = norm_mod(x_ref[...])
    h_prev = jnp.where(start, 0.0, norm_mod(xp_ref[...])[7:8])
    h_next = jnp.where(end, 0.0, norm_mod(xn_ref[...])[0:1])
    row = lax.broadcasted_iota(jnp.int32, (TM, D), 0)
    h_dn = jnp.where(row == 0, h_prev, pltpu.roll(h, 1, 0))
    h_up = jnp.where(row == TM - 1, h_next, pltpu.roll(h, TM - 1, 0))
    xx = 0.5 * (h_dn + h_up) - h
    mix = mix_ref[...]
    xr, xw, xk, xv, xa, xg = (h + xx * mix[i:i + 1] for i in range(6))

    r = _mm(xr, wr_ref[...])
    k = _mm(xk, wk_ref[...])
    v = _mm(xv, wv_ref[...])
    g = _mm(jax.nn.sigmoid(_mm(xg, g1_ref[...])), g2_ref[...])
    wl = w0_ref[...] + _mm(jnp.tanh(_mm(xw, w1_ref[...])), w2_ref[...])
    log_decay = -jnp.exp(-0.5) * jax.nn.sigmoid(wl)
    alpha = jax.nn.sigmoid(a0_ref[...] + _mm(_mm(xa, a1_ref[...]), a2_ref[...]))

    ones = _seg_ones()
    kk = k * kk_ref[...]
    kk = kk * lax.rsqrt(jnp.maximum(_seg_sum(kk * kk, ones), 1e-24))
    r_out[...] = r
    v_out[...] = v
    kn_out[...] = kk
    g_out[...] = g
    kd_sum = jnp.zeros((TM, D), F32)
    for d in range(2):
        al = alpha[:, d * D:(d + 1) * D]
        kd = k * (1.0 + (al - 1.0) * ka_ref[...])
        kd_sum = kd_sum + kd
        lw_out[d] = log_decay[:, d * D:(d + 1) * D]
        kd_out[d] = kd
        al_out[d] = al
    bonus_out[...] = _seg_sum(r * kd_sum * rk_ref[...], ones) * v


def _rwkv_in_call(x, mods_l, g, mix, wr, wk, wv, g1, g2, w1, w2, w0, a1, a2, a0, k_k, k_a, r_k):
    per = TM // 8
    last = N_TOK // 8 - 1
    tok_out = jax.ShapeDtypeStruct((N_TOK, D), F32)
    dir_out = jax.ShapeDtypeStruct((2, N_TOK, D), F32)
    dir_spec = pl.BlockSpec((2, TM, D), lambda i: (0, i, 0))
    return pl.pallas_call(
        _rwkv_in_kernel,
        out_shape=(tok_out,) * 5 + (dir_out,) * 3,
        grid=(N_TOK // TM,),
        in_specs=[_tok(D),
                  pl.BlockSpec((8, D), lambda i: (jnp.maximum(i * per - 1, 0), 0)),
                  pl.BlockSpec((8, D), lambda i: (jnp.minimum((i + 1) * per, last), 0)),
                  _modspec(), _full((1, D)), _full((6, D)),
                  _full((D, D)), _full((D, D)), _full((D, D)), _full((D, GATE_LORA)), _full((GATE_LORA, D)),
                  _full((D, 2 * LORA)), _full((2 * LORA, 2 * D)), _full((1, 2 * D)),
                  _full((D, 2 * LORA)), _full((2 * LORA, 2 * D)), _full((1, 2 * D)),
                  _full((1, D)), _full((1, D)), _full((1, D))],
        out_specs=(_tok(D),) * 5 + (dir_spec,) * 3,
        compiler_params=_params(),
        name="rwkv_in",
    )(x, x, x, mods_l, _row(g), mix, wr, wk, wv, g1, g2, w1, w2, _row(w0), a1, a2, _row(a0),
      _row(k_k), _row(k_a), _row(r_k))


def _block_diag_rows(z):
    zb = z.astype(BF16)
    lane_head = lax.broadcasted_iota(jnp.int32, zb.shape, 1) // HD
    zero = jnp.zeros_like(zb)
    return jnp.concatenate([jnp.where(lane_head == h, zb, zero) for h in range(QUAD // HD)], axis=0)


def _wkv_prep(rev, r_ref, v_ref, kn_ref, lw_ref, kd_ref, al_ref):
    C = WKV_CHUNK
    row = lax.broadcasted_iota(jnp.int32, (C, C), 0)
    col = lax.broadcasted_iota(jnp.int32, (C, C), 1)
    incl = (col >= row) if rev else (col <= row)
    lw = lw_ref[0]
    hi, mid, lo = _split3(lw)
    inc_b = incl.astype(BF16)
    dot = functools.partial(jnp.dot, preferred_element_type=F32)
    g_incl = dot(inc_b, hi) + (dot(inc_b, mid) + dot(inc_b, lo))
    g_tot = jnp.sum(lw, axis=0, keepdims=True)
    kn = kn_ref[...]
    b = kn * al_ref[0]
    kd = kd_ref[0]
    e_neg = jnp.exp(-g_incl)
    e_rem = jnp.exp(g_tot - g_incl)
    return dict(a_t=-kn * jnp.exp(g_incl - lw), r_t=r_ref[...] * jnp.exp(g_incl), b_t=b * e_neg, k_t=kd * e_neg,
                b_h=b * e_rem, k_h=kd * e_rem, w_tot=jnp.exp(g_tot), v=v_ref[...])


def _wkv_kernel(*refs, n_chunks, has_init):
    fwd_refs, bwd_refs, rest = refs[0:6], refs[6:12], refs[12:]
    if has_init:
        s0_ref, yf_ref, yb_ref, sfin_ref, s_ref = rest
    else:
        yf_ref, yb_ref, sfin_ref, s_ref = rest
    c = pl.program_id(1)
    C = WKV_CHUNK

    @pl.when(c == 0)
    def _():
        if has_init:
            for d in range(2):
                for h in range(RW_HEADS):
                    s_ref[d, :, h * HD:(h + 1) * HD] = s0_ref[0, d, h]
        else:
            s_ref[...] = jnp.zeros_like(s_ref)

    row4 = lax.broadcasted_iota(jnp.int32, (C, QUAD), 0)
    col4 = lax.broadcasted_iota(jnp.int32, (C, QUAD), 1) % C
    r2 = lax.broadcasted_iota(jnp.int32, (QUAD, QUAD), 0)
    c2 = lax.broadcasted_iota(jnp.int32, (QUAD, QUAD), 1)
    same_head = (r2 // HD) == (c2 // HD)
    strict4 = (col4 < row4, col4 > row4)
    incl4 = (col4 <= row4, col4 >= row4)

    prep = (_wkv_prep(False, *fwd_refs), _wkv_prep(True, *bwd_refs))
    y_refs = (yf_ref, yb_ref)
    chains = [(d, slice(q * QUAD, (q + 1) * QUAD)) for d in range(2) for q in range(D // QUAD)]

    def rows(name):
        return [prep[d][name][:, sl] for d, sl in chains]

    a_t, r_t, b_t, k_t, b_h, k_h, w_tot, v = (rows(n) for n in ("a_t", "r_t", "b_t", "k_t", "b_h", "k_h", "w_tot", "v"))
    s_in = [s_ref[d, :, sl] for d, sl in chains]
    bd = _block_diag_rows
    dot = functools.partial(jnp.dot, preferred_element_type=F32)
    cat = functools.partial(jnp.concatenate, axis=0)
    ar = [cat([a, r]) for a, r in zip(a_t, r_t)]
    gram_b = [_mm_nt(z, bd(b)) for z, b in zip(ar, b_t)]
    gram_k = [_mm_nt(z, bd(k)) for z, k in zip(ar, k_t)]
    a_ab = [jnp.where(strict4[d], g[:C], 0.0) for (d, _), g in zip(chains, gram_b)]
    a_ak = [jnp.where(strict4[d], g[:C], 0.0) for (d, _), g in zip(chains, gram_k)]
    a_rb = [jnp.where(incl4[d], g[C:], 0.0) for (d, _), g in zip(chains, gram_b)]
    a_rk = [jnp.where(incl4[d], g[C:], 0.0) for (d, _), g in zip(chains, gram_k)]
    H = C // 2
    diag_blk = (row4 // H) == (col4 // H)
    l_d = [jnp.where(diag_blk, l, 0.0) for l in a_ab]
    l_o = [jnp.where(diag_blk, 0.0, l) for l in a_ab]
    rowh = lax.broadcasted_iota(jnp.int32, (H, QUAD), 0)
    laneh = lax.broadcasted_iota(jnp.int32, (H, QUAD), 1)
    eye8 = ((laneh % H) == rowh).astype(F32)

    def bd8(z):
        grp = lax.broadcasted_iota(jnp.int32, z.shape, 1) // H
        zero = jnp.zeros_like(z)
        return cat([jnp.where(grp == g, z, zero) for g in range(QUAD // H)])

    p = [z[:H] + z[H:] for z in l_d]
    x = [eye8 + z for z in p]
    n_steps = 5
    for i in range(n_steps):
        w = [bd8(pq.astype(BF16)) for pq in p]
        if i == 0:
            p = [dot(pq.astype(BF16), wq) for pq, wq in zip(p, w)]
        elif i < n_steps - 1:
            r1 = [dot(cat([pq, xq]).astype(BF16), wq) for pq, xq, wq in zip(p, x, w)]
            p = [a[:H] for a in r1]
            x = [xq + a[H:] for xq, a in zip(x, r1)]
        else:
            x = [xq + dot(xq.astype(BF16), wq) for xq, wq in zip(x, w)]
    top = (laneh % C) < H
    t_d = [cat([jnp.where(top, xq, 0.0), jnp.where(top, 0.0, xq)]) for xq in x]
    z = [_mm(t, bd(lo_)) for t, lo_ in zip(t_d, l_o)]
    x = [t + _mm(zq, bd(t)) for t, zq in zip(t_d, z)]
    vbd = [bd(z) for z in v]
    rv = [_mm(cat([m, n]), vb) for m, n, vb in zip(a_ak, a_rk, vbd)]
    akv = [z[:C] for z in rv]
    wm = [_mm(xq, bd(a)) for xq, a in zip(x, a_t)]
    um = [_mm(xq, bd(z)) for xq, z in zip(x, akv)]
    qm = [r + _mm(m, bd(w)) for r, m, w in zip(r_t, a_rb, wm)]
    y0 = [_mm(m, bd(z)) + z2[C:] for m, z, z2 in zip(a_rb, um, rv)]
    rs = [_mm_nt(cat([w, qq]), bd(sq)) for w, qq, sq in zip(wm, qm, s_in)]
    u = [z[:C] + z2 for z, z2 in zip(rs, um)]
    for (d, sl), z, yy in zip(chains, rs, y0):
        y_refs[d][:, sl] = z[C:] + yy
    for (d, sl), uq, vq, bq, kq, wq, sq in zip(chains, u, v, b_h, k_h, w_tot, s_in):
        upd = _mm_tn(jnp.concatenate([uq, vq], axis=0), jnp.concatenate([bq, kq], axis=0))
        upd = jnp.where(same_head, upd, 0.0)
        upd = (upd[0:HD] + upd[HD:2 * HD]) + (upd[2 * HD:3 * HD] + upd[3 * HD:])
        s_ref[d, :, sl] = sq * wq + upd

    @pl.when(c == n_chunks - 1)
    def _():
        for d in range(2):
            for h in range(RW_HEADS):
                sfin_ref[0, d, h] = s_ref[d, :, h * HD:(h + 1) * HD]


def _wkv_call(r, v, kn, lw, kd, al, n_seq, seq_len, row_off, s0=None):
    C = WKV_CHUNK
    n_chunks = seq_len // C
    off = row_off // C

    def fwd(s, c):
        return s * n_chunks + c

    def bwd(s, c):
        return s * n_chunks + n_chunks - 1 - c

    def specs(blk, d):
        tok = pl.BlockSpec((C, D), lambda s, c: (off + blk(s, c), 0))
        dirtok = pl.BlockSpec((1, C, D), lambda s, c: (d, off + blk(s, c), 0))
        return [tok, tok, tok, dirtok, dirtok, dirtok]

    state = pl.BlockSpec((1, 2, RW_HEADS, HD, HD), lambda s, c: (s, 0, 0, 0, 0))
    has_init = s0 is not None
    in_specs = specs(fwd, 0) + specs(bwd, 1) + ([state] if has_init else [])
    args = [r, v, kn, lw, kd, al] * 2 + ([s0] if has_init else [])
    y_shape = jax.ShapeDtypeStruct((n_seq * seq_len, D), F32)
    return pl.pallas_call(
        functools.partial(_wkv_kernel, n_chunks=n_chunks, has_init=has_init),
        out_shape=(y_shape, y_shape, jax.ShapeDtypeStruct((n_seq, 2, RW_HEADS, HD, HD), F32)),
        grid=(n_seq, n_chunks),
        in_specs=in_specs,
        out_specs=(pl.BlockSpec((C, D), lambda s, c: (fwd(s, c), 0)),
                   pl.BlockSpec((C, D), lambda s, c: (bwd(s, c), 0)), state),
        scratch_shapes=[pltpu.VMEM((2, HD, D), F32)],
        compiler_params=_params(2),
        name="wkv_scan",
    )(*args)


def _rwkv_out_kernel(yfc_ref, yfl_ref, ybc_ref, ybl_ref, bonus_ref, g_ref, x_ref, mod_ref, lg_ref, lb_ref, wo_ref,
                     o_ref):
    ones = _seg_ones()
    y = _ctx_or_lat(yfc_ref, yfl_ref) + _ctx_or_lat(ybc_ref, ybl_ref)
    mu = _seg_sum(y, ones) * (1.0 / HD)
    yc = y - mu
    var = _seg_sum(yc * yc, ones) * (1.0 / HD)
    out = yc * lax.rsqrt(var + RW_GN_EPS) * lg_ref[...] + lb_ref[...] + bonus_ref[...]
    o_ref[...] = x_ref[...] + mod_ref[0][5:6] * _mm(out * g_ref[...], wo_ref[...])


def _rwkv_out_call(y_ctx, y_lat, bonus, g, x, mods_l, lg, lb, wo):
    return pl.pallas_call(
        _rwkv_out_kernel,
        out_shape=jax.ShapeDtypeStruct((N_TOK, D), F32),
        grid=(N_TOK // TM,),
        in_specs=[*_ctx_lat_specs((TM, D)), *_ctx_lat_specs((TM, D)), _tok(D), _tok(D), _tok(D), _modspec(),
                  _full((1, D)), _full((1, D)), _full((D, D))],
        out_specs=_tok(D),
        compiler_params=_params(),
        name="rwkv_out",
    )(y_ctx[0], y_lat[0], y_ctx[1], y_lat[1], bonus, g, x, mods_l, _row(lg), _row(lb), wo)


def _rope_tables():
    t = jnp.arange(DEC_SEQ)
    rowp = (t // GRID_W).astype(F32)
    colp = (t % GRID_W).astype(F32)
    axis_dim = HD // 2
    freqs = ROPE_THETA ** (-jnp.arange(0, axis_dim, 2, dtype=F32) / axis_dim)
    ang = jnp.concatenate([rowp[:, None] * freqs, colp[:, None] * freqs], axis=-1)
    cos = jnp.repeat(jnp.cos(ang), 2, axis=-1)
    sin = jnp.repeat(jnp.sin(ang), 2, axis=-1) * jnp.tile(jnp.array([-1.0, 1.0], F32), HD // 2)
    cos = jnp.concatenate([jnp.ones((TM, HD), F32), cos], axis=0)
    sin = jnp.concatenate([jnp.zeros((TM, HD), F32), sin], axis=0)
    return jnp.tile(cos, (1, LANES // HD)), jnp.tile(sin, (1, LANES // HD))


def _block_diag2(w):
    z = jnp.zeros_like(w[0])
    return jnp.concatenate([jnp.concatenate([w[0], z], axis=1), jnp.concatenate([z, w[1]], axis=1)], axis=0)


def kernel(x_prompt, x_sample, c, c_ctx, cache_k0, cache_v0, cache_k2, cache_v2, state_wkv3,
           norm_g, mod_w, mod_b, ffn_w_in, ffn_w_down, final_norm_g,
           gq_w_qkv, gq_q_norm, gq_k_norm, gq_w_o,
           cv_w_in, cv_b_in, cv_w_dw, cv_b_dw, cv_ln_g, cv_ln_b, cv_w_out, cv_b_out,
           df_w_qkv, df_lambda_q1, df_lambda_k1, df_lambda_q2, df_lambda_k2, df_subln_g, df_w_o,
           rw_mix, rw_w_r, rw_w_k, rw_w_v, rw_w_o, rw_k_k, rw_k_a, rw_r_k, rw_g1, rw_g2,
           rw_ln_g, rw_ln_b, rw_w0, rw_w1, rw_w2, rw_a0, rw_a1, rw_a2):
    bf = lambda w: w.astype(BF16)
    cond8 = jnp.concatenate([c_ctx[None, :], c, jnp.zeros((8 - 1 - DEC_BATCH, D), F32)], axis=0)
    mods = _mods_call(cond8, mod_w, mod_b)
    cos_t, sin_t = _rope_tables()

    w_in_all, w_down_all = bf(ffn_w_in), bf(ffn_w_down)

    def ffn(x, layer, which, final_g=None):
        return _ffn_call(x, mods[layer], norm_g[layer, 2 * which], w_in_all, w_down_all, layer, which, final_g)

    x = ffn((x_prompt.reshape(N_CTX, D), x_sample.reshape(N_LAT, D)), 0, 0)
    q, kc, kl, vc, vl = _gqa_proj_call(x, mods[0], norm_g[0, 1], bf(gq_w_qkv), gq_q_norm, gq_k_norm, cos_t, sin_t)
    new_k0 = kc.reshape(BATCH, SEQ, GQ_KV, HD)
    new_v0 = vc.reshape(BATCH, SEQ, GQ_KV, HD)
    oc = _gqa_attn_call(q, kc, vc)
    ol = _gqa_attn_call(q, kl, vl, cache_k0.reshape(DEC_BATCH * PAST, GQ_KVW),
                        cache_v0.reshape(DEC_BATCH * PAST, GQ_KVW))
    x = _out_call(oc, ol, x, mods[0], bf(gq_w_o))
    x = ffn(x, 0, 1)

    x = ffn(x, 1, 0)
    u = _conv_in_call(x, mods[1], norm_g[1, 1], bf(cv_w_in), cv_b_in)
    x = _conv_out_call(u, x, mods[1], cv_w_dw, cv_b_dw, cv_ln_g, cv_ln_b, bf(cv_w_out), cv_b_out)
    x = ffn(x, 1, 1)

    x = ffn(x, 2, 0)
    q, kc, kl, vc, vl = _diff_proj_call(x, mods[2], norm_g[2, 1], bf(df_w_qkv), cos_t, sin_t)
    new_k2 = kc.reshape(BATCH, SEQ, DF_HEADS, 2, HD)
    new_v2 = vc.reshape(BATCH, SEQ, DF_HEADS, 2 * HD)
    lam_rows = jnp.stack([df_lambda_q1, df_lambda_k1, df_lambda_q2, df_lambda_k2]).astype(F32)
    oc = _diff_attn_call(q, kc, vc, lam_rows, df_subln_g)
    ol = _diff_attn_call(q, kl, vl, lam_rows, df_subln_g, cache_k2.reshape(DEC_BATCH * PAST, D),
                         cache_v2.reshape(DEC_BATCH * PAST, D))
    x = _out_call(oc, ol, x, mods[2], bf(df_w_o))
    x = ffn(x, 2, 1)

    x = ffn(x, 3, 0)
    w1 = bf(jnp.concatenate([rw_w1[0], rw_w1[1]], axis=1))
    a1 = bf(jnp.concatenate([rw_a1[0], rw_a1[1]], axis=1))
    r, v, kn, g, bonus, lw, kd, al = _rwkv_in_call(
        x, mods[3], norm_g[3, 1], rw_mix, bf(rw_w_r), bf(rw_w_k), bf(rw_w_v), bf(rw_g1), bf(rw_g2),
        w1, bf(_block_diag2(rw_w2)), rw_w0.reshape(-1), a1, bf(_block_diag2(rw_a2)), rw_a0.reshape(-1),
        rw_k_k, rw_k_a, rw_r_k.reshape(-1))
    *y_ctx, new_wkv3 = _wkv_call(r, v, kn, lw, kd, al, BATCH, SEQ, 0)
    *y_lat, _ = _wkv_call(r, v, kn, lw, kd, al, DEC_BATCH, DEC_SEQ, N_CTX, state_wkv3)
    x = _rwkv_out_call(y_ctx, y_lat, bonus, g, x, mods[3], rw_ln_g, rw_ln_b, bf(rw_w_o))
    y_prompt, y_sample = ffn(x, 3, 1, final_norm_g)
    y_prompt = y_prompt.reshape(BATCH, SEQ, D)
    y_sample = y_sample.reshape(DEC_BATCH, DEC_SEQ, D)
    return (y_prompt, y_sample, new_k0, new_v0, new_k2, new_v2, new_wkv3)
```

```python
import functools

import jax
import jax.numpy as jnp
from jax import lax
from jax.experimental import pallas as pl
from jax.experimental.pallas import tpu as pltpu

F32 = jnp.float32
BF16 = jnp.bfloat16

D = 1024
BATCH, SEQ = 32, 256
DEC_BATCH, DEC_SEQ = 2, 2048
PAST = 512
DEPTH = 4
GRID_W = 64
N_MOD = 9
NORM_EPS = 1e-6
LN_EPS = 1e-5
ROPE_THETA = 10000.0
D_FF = 2816
HD = 64
GQ_HEADS, GQ_KV = 16, 4
GQ_GROUP = GQ_HEADS // GQ_KV
GQ_KVW = GQ_KV * HD
CV_WIDTH = 31
CV_PAD = CV_WIDTH // 2
DF_HEADS = 8
DF_LAMBDA_INIT = 0.470713018
DF_SUBLN_EPS = 1e-5
RW_HEADS = 16
RW_GN_EPS = 64e-5
LORA = 64
GATE_LORA = 128

N_CTX = BATCH * SEQ
N_LAT = DEC_BATCH * DEC_SEQ
N_TOK = N_CTX + N_LAT

LANES = 128
TM = 256
TM_FFN = 512
FF_CHUNK = 256
WKV_CHUNK = 64
QUAD = 4 * HD
VMEM_LIMIT = 56 * 1024 * 1024


def _mm(a, b):
    return jnp.dot(a.astype(BF16), b.astype(BF16), preferred_element_type=F32)


def _mm_nt(a, b):
    return lax.dot_general(a.astype(BF16), b.astype(BF16), (((1,), (1,)), ((), ())),
                           preferred_element_type=F32)


def _mm_tn(a, b):
    return lax.dot_general(a.astype(BF16), b.astype(BF16), (((0,), (0,)), ((), ())),
                           preferred_element_type=F32)


def _split3(x):
    hi = x.astype(BF16)
    r1 = x - hi.astype(F32)
    mid = r1.astype(BF16)
    lo = (r1 - mid.astype(F32)).astype(BF16)
    return hi, mid, lo


def _split2(x):
    hi = x.astype(BF16)
    return hi, (x - hi.astype(F32)).astype(BF16)


def _silu(x):
    return x * jax.nn.sigmoid(x)


def _rms(x, g, eps=NORM_EPS):
    return x * lax.rsqrt(jnp.mean(x * x, axis=-1, keepdims=True) + eps) * g


def _seg_ones():
    r = lax.broadcasted_iota(jnp.int32, (LANES, LANES), 0) // HD
    c = lax.broadcasted_iota(jnp.int32, (LANES, LANES), 1) // HD
    return (r == c).astype(BF16)


def _seg_sum(x, ones):
    outs = []
    for g in range(x.shape[1] // LANES):
        hi, lo = _split2(x[:, g * LANES:(g + 1) * LANES])
        dot = functools.partial(jnp.dot, preferred_element_type=F32)
        outs.append(dot(hi, ones) + dot(lo, ones))
    return outs[0] if len(outs) == 1 else jnp.concatenate(outs, axis=1)


def _rope(x, cos, sin_s):
    lane = lax.broadcasted_iota(jnp.int32, (x.shape[0], LANES), 1)
    even = (lane % 2) == 0
    outs = []
    for g in range(x.shape[1] // LANES):
        xg = x[:, g * LANES:(g + 1) * LANES]
        partner = jnp.where(even, pltpu.roll(xg, LANES - 1, 1), pltpu.roll(xg, 1, 1))
        outs.append(xg * cos + partner * sin_s)
    return outs[0] if len(outs) == 1 else jnp.concatenate(outs, axis=1)


def _cond_row(i, tm):
    nt_ctx = N_CTX // tm
    return jnp.where(i < nt_ctx, 0, 1 + (i - nt_ctx) // (DEC_SEQ // tm))


def _tok(width, tm=TM):
    return pl.BlockSpec((tm, width), lambda i: (i, 0))


def _full(shape):
    nd = len(shape)
    return pl.BlockSpec(tuple(shape), lambda i: (0,) * nd, pipeline_mode=pl.Buffered(1))


def _modspec(tm=TM):
    return pl.BlockSpec((1, N_MOD, D), lambda i: (_cond_row(i, tm), 0, 0))


def _ropespec():
    nt_ctx = N_CTX // TM
    per_seq = DEC_SEQ // TM
    return pl.BlockSpec((TM, LANES), lambda i: (jnp.where(i < nt_ctx, 0, 1 + (i - nt_ctx) % per_seq), 0))


def _params(n_axes=1):
    return pltpu.CompilerParams(dimension_semantics=("arbitrary",) * n_axes, vmem_limit_bytes=VMEM_LIMIT)


def _row(v):
    return v.reshape(1, -1).astype(F32)


def _ctx_lat_specs(block, tm=TM):
    nt_ctx = N_CTX // tm
    lead = (0,) * (len(block) - 2)
    return (pl.BlockSpec(block, lambda i: lead + (jnp.minimum(i, nt_ctx - 1), 0)),
            pl.BlockSpec(block, lambda i: lead + (jnp.maximum(i - nt_ctx, 0), 0)))


def _ctx_lat_shapes(width):
    return (jax.ShapeDtypeStruct((N_CTX, width), F32), jax.ShapeDtypeStruct((N_LAT, width), F32))


def _ctx_or_lat(c_ref, l_ref, tm=TM):
    return jnp.where(pl.program_id(0) < N_CTX // tm, c_ref[...], l_ref[...])


def _store_ctx_lat(c_ref, l_ref, val, tm=TM):
    l_ref[...] = val

    @pl.when(pl.program_id(0) < N_CTX // tm)
    def _():
        c_ref[...] = val


def _mods_kernel(c_ref, w_ref, b_ref, o_ref):
    o_ref[0] = _mm(_silu(c_ref[...]), w_ref[0]) + b_ref[0]


def _mods_call(cond8, mod_w, mod_b):
    out = pl.pallas_call(
        _mods_kernel,
        out_shape=jax.ShapeDtypeStruct((DEPTH, 8, N_MOD * D), F32),
        grid=(DEPTH, N_MOD),
        in_specs=[pl.BlockSpec((8, D), lambda l, j: (0, 0)),
                  pl.BlockSpec((1, D, D), lambda l, j: (l, 0, j)),
                  pl.BlockSpec((1, 1, D), lambda l, j: (l, 0, j))],
        out_specs=pl.BlockSpec((1, 8, D), lambda l, j: (l, 0, j)),
        compiler_params=_params(2),
        name="adaln_mods",
    )(cond8, mod_w, mod_b.reshape(DEPTH, 1, N_MOD * D))
    return out.reshape(DEPTH, 8, N_MOD, D)


N_STAGE = 11
IN_COLS = 2 * D_FF // N_STAGE
DOWN_ROWS = D_FF // N_STAGE


def _stage_weights(win_hbm, wd_hbm, win_ref, wd_ref, in_buf, dn_buf, sem, layer, which):
    def in_copy(c, slot):
        return pltpu.make_async_copy(win_hbm.at[layer, which, :, pl.ds(c * IN_COLS, IN_COLS)],
                                     in_buf.at[slot], sem.at[0, slot])

    def dn_copy(c, slot):
        return pltpu.make_async_copy(wd_hbm.at[layer, which, pl.ds(c * DOWN_ROWS, DOWN_ROWS), :],
                                     dn_buf.at[slot], sem.at[1, slot])

    in_copy(0, 0).start()
    dn_copy(0, 0).start()
    for c in range(N_STAGE):
        slot = c % 2
        if c + 1 < N_STAGE:
            in_copy(c + 1, 1 - slot).start()
            dn_copy(c + 1, 1 - slot).start()
        in_copy(c, slot).wait()
        win_ref[:, c * IN_COLS:(c + 1) * IN_COLS] = in_buf[slot].astype(BF16)
        dn_copy(c, slot).wait()
        wd_ref[c * DOWN_ROWS:(c + 1) * DOWN_ROWS, :] = dn_buf[slot].astype(BF16)


def _ffn_kernel(*refs, row0, layer, which, first, final):
    refs = list(refs)
    x = _ctx_or_lat(refs.pop(0), refs.pop(0), TM_FFN) if first else refs.pop(0)[...]
    mod_ref, g_ref, win_hbm, wd_hbm = refs[:4]
    act_ref, win_ref, wd_ref, in_buf, dn_buf, sem = refs[-6:]

    @pl.when(pl.program_id(0) == 0)
    def _():
        _stage_weights(win_hbm, wd_hbm, win_ref, wd_ref, in_buf, dn_buf, sem, layer, which)

    m = mod_ref[0]
    h = _rms(x, g_ref[...]) * (1.0 + m[row0 + 1:row0 + 2]) + m[row0:row0 + 1]
    hb = h.astype(BF16)
    for c in range(D_FF // FF_CHUNK):
        lo = c * FF_CHUNK
        gate = jnp.dot(hb, win_ref[:, lo:lo + FF_CHUNK], preferred_element_type=F32)
        up = jnp.dot(hb, win_ref[:, D_FF + lo:D_FF + lo + FF_CHUNK], preferred_element_type=F32)
        act_ref[:, lo:lo + FF_CHUNK] = (_silu(gate) * up).astype(BF16)
    out = jnp.dot(act_ref[...], wd_ref[...], preferred_element_type=F32)
    y = x + (0.5 * m[row0 + 2:row0 + 3]) * out
    if final:
        fg_ref, oc_ref, ol_ref = refs[4:7]
        _store_ctx_lat(oc_ref, ol_ref, _rms(y, fg_ref[...]), TM_FFN)
    else:
        refs[4][...] = y


def _ffn_call(x, mods_l, g, w_in, w_down, layer, which, final_g=None):
    first = isinstance(x, tuple)
    final = final_g is not None
    hbm = pl.BlockSpec(memory_space=pl.ANY)
    in_specs = (list(_ctx_lat_specs((TM_FFN, D), TM_FFN)) if first else [_tok(D, TM_FFN)]) + [
        _modspec(TM_FFN), _full((1, D)), hbm, hbm]
    args = (list(x) if first else [x]) + [mods_l, _row(g), w_in, w_down]
    if final:
        in_specs.append(_full((1, D)))
        args.append(_row(final_g))
        out_shape = (jax.ShapeDtypeStruct((N_CTX, D), F32), jax.ShapeDtypeStruct((N_LAT, D), F32))
        out_specs = _ctx_lat_specs((TM_FFN, D), TM_FFN)
    else:
        out_shape = jax.ShapeDtypeStruct((N_TOK, D), F32)
        out_specs = _tok(D, TM_FFN)
    return pl.pallas_call(
        functools.partial(_ffn_kernel, row0=6 * which, layer=layer, which=which, first=first, final=final),
        out_shape=out_shape,
        grid=(N_TOK // TM_FFN,),
        in_specs=in_specs,
        out_specs=out_specs,
        scratch_shapes=[pltpu.VMEM((TM_FFN, D_FF), BF16), pltpu.VMEM((D, 2 * D_FF), BF16), pltpu.VMEM((D_FF, D), BF16),
                        pltpu.VMEM((2, D, IN_COLS), F32), pltpu.VMEM((2, DOWN_ROWS, D), F32),
                        pltpu.SemaphoreType.DMA((2, 2))],
        compiler_params=_params(),
        name="ffn_final" if final else "ffn",
    )(*args)


def _out_kernel(oc_ref, ol_ref, x_ref, mod_ref, w_ref, y_ref):
    out = _mm(_ctx_or_lat(oc_ref, ol_ref), w_ref[...])
    y_ref[...] = x_ref[...] + mod_ref[0][5:6] * out


def _out_call(oc, ol, x, mods_l, w):
    return pl.pallas_call(
        _out_kernel,
        out_shape=jax.ShapeDtypeStruct((N_TOK, D), F32),
        grid=(N_TOK // TM,),
        in_specs=[*_ctx_lat_specs((TM, oc.shape[1])), _tok(D), _modspec(), _full(w.shape)],
        out_specs=_tok(D),
        compiler_params=_params(),
        name="out_proj",
    )(oc, ol, x, mods_l, w)


def _gqa_proj_kernel(x_ref, mod_ref, g_ref, w_ref, qn_ref, kn_ref, cos_ref, sin_ref,
                     q_ref, kc_ref, kl_ref, vc_ref, vl_ref):
    m = mod_ref[0]
    h = _rms(x_ref[...], g_ref[...]) * (1.0 + m[4:5]) + m[3:4]
    qkv = _mm(h, w_ref[...])
    ones = _seg_ones()
    cos, sin_s = cos_ref[...], sin_ref[...]
    q = qkv[:, :D]
    k = qkv[:, D:D + GQ_KVW]
    q = q * lax.rsqrt(_seg_sum(q * q, ones) * (1.0 / HD) + NORM_EPS) * qn_ref[...]
    k = k * lax.rsqrt(_seg_sum(k * k, ones) * (1.0 / HD) + NORM_EPS) * kn_ref[...]
    q_ref[...] = _rope(q, cos, sin_s)
    _store_ctx_lat(kc_ref, kl_ref, _rope(k, cos, sin_s))
    _store_ctx_lat(vc_ref, vl_ref, qkv[:, D + GQ_KVW:])


def _gqa_proj_call(x, mods_l, g, w, qn, kn, cos_t, sin_t):
    return pl.pallas_call(
        _gqa_proj_kernel,
        out_shape=(jax.ShapeDtypeStruct((N_TOK, D), F32),) + _ctx_lat_shapes(GQ_KVW) * 2,
        grid=(N_TOK // TM,),
        in_specs=[_tok(D), _modspec(), _full((1, D)), _full(w.shape), _full((1, D)), _full((1, GQ_KVW)),
                  _ropespec(), _ropespec()],
        out_specs=(_tok(D),) + _ctx_lat_specs((TM, GQ_KVW)) * 2,
        compiler_params=_params(),
        name="gqa_proj",
    )(x, mods_l, _row(g), w, _row(jnp.tile(qn, GQ_HEADS)), _row(jnp.tile(kn, GQ_KV)), cos_t, sin_t)


def _softmax_parts(q, ks):
    ss = [_mm_nt(q, k) for k in ks]
    mx = functools.reduce(jnp.maximum, [jnp.max(s, axis=-1, keepdims=True) for s in ss])
    es = [jnp.exp(s - mx) for s in ss]
    return es, functools.reduce(jnp.add, [jnp.sum(e, axis=-1, keepdims=True) for e in es])


def _attn_specs(latent, kv_width):
    if not latent:
        tok = pl.BlockSpec((SEQ, D), lambda i: (i, 0))
        kv = pl.BlockSpec((SEQ, kv_width), lambda i: (i, 0))
        return (BATCH,), tok, [kv], tok
    per = DEC_SEQ // TM
    q = pl.BlockSpec((TM, D), lambda b, j: (N_CTX // TM + b * per + j, 0))
    cache = pl.BlockSpec((PAST, kv_width), lambda b, j: (b, 0))
    own = pl.BlockSpec((DEC_SEQ, kv_width), lambda b, j: (b, 0))
    out = pl.BlockSpec((TM, D), lambda b, j: (b * per + j, 0))
    return (DEC_BATCH, per), q, [cache, own], out


def _gqa_attn_kernel(q_ref, *refs):
    o_ref = refs[-1]
    k_refs, v_refs = refs[0:-1:2], refs[1:-1:2]
    scale = HD ** -0.5
    for kv in range(GQ_KV):
        cols = slice(kv * HD, (kv + 1) * HD)
        ks = [r[:, cols] for r in k_refs]
        vs = [r[:, cols].astype(BF16) for r in v_refs]
        for g in range(GQ_GROUP):
            h = kv * GQ_GROUP + g
            es, l = _softmax_parts(q_ref[:, h * HD:(h + 1) * HD] * scale, ks)
            pv = functools.reduce(jnp.add, [_mm(e, v) for e, v in zip(es, vs)])
            o_ref[:, h * HD:(h + 1) * HD] = pv / l


def _gqa_attn_call(q, k, v, cache_k=None, cache_v=None):
    latent = cache_k is not None
    grid, q_spec, kv_specs, o_spec = _attn_specs(latent, GQ_KVW)
    parts = [(cache_k, cache_v), (k, v)] if latent else [(k, v)]
    return pl.pallas_call(
        _gqa_attn_kernel,
        out_shape=jax.ShapeDtypeStruct((N_LAT if latent else N_CTX, D), F32),
        grid=grid,
        in_specs=[q_spec] + [s for s in kv_specs for _ in range(2)],
        out_specs=o_spec,
        compiler_params=_params(len(grid)),
        name="gqa_attn",
    )(q, *[a for part in parts for a in part])


def _conv_in_kernel(x_ref, mod_ref, g_ref, w_ref, b_ref, u_ref):
    m = mod_ref[0]
    h = _rms(x_ref[...], g_ref[...]) * (1.0 + m[4:5]) + m[3:4]
    u = _mm(h, w_ref[...]) + b_ref[...]
    u_ref[...] = u[:, :D] * jax.nn.sigmoid(u[:, D:])


def _conv_in_call(x, mods_l, g, w, b):
    return pl.pallas_call(
        _conv_in_kernel,
        out_shape=jax.ShapeDtypeStruct((N_TOK, D), F32),
        grid=(N_TOK // TM,),
        in_specs=[_tok(D), _modspec(), _full((1, D)), _full(w.shape), _full((1, 2 * D))],
        out_specs=_tok(D),
        compiler_params=_params(),
        name="conv_in",
    )(x, mods_l, _row(g), w, _row(b))


HALO = 16


def _seq_edges(i):
    nt_ctx = N_CTX // TM
    per = jnp.where(i < nt_ctx, SEQ // TM, DEC_SEQ // TM)
    j = jnp.where(i < nt_ctx, i, i - nt_ctx) % per
    return j == 0, j == per - 1


def _conv_out_kernel(u_ref, up_ref, un_ref, x_ref, mod_ref, wdw_ref, bdw_ref, lg_ref, lb_ref, wo_ref, bo_ref,
                     y_ref, pad_ref):
    start, end = _seq_edges(pl.program_id(0))
    pad_ref[0:HALO, :] = jnp.where(start, 0.0, up_ref[...])
    pad_ref[HALO:HALO + TM, :] = u_ref[...]
    pad_ref[HALO + TM:, :] = jnp.where(end, 0.0, un_ref[...])
    wdw = wdw_ref[...]
    pad = pad_ref[...]
    n_pad = TM + 2 * HALO
    acc = jnp.zeros((TM, D), F32)
    for phase in range(8):
        shifted = pad if phase == 0 else pltpu.roll(pad, n_pad - phase, 0)
        for j in range(CV_WIDTH):
            off = HALO - CV_PAD + j
            if off % 8 == phase:
                acc = acc + wdw[j:j + 1] * shifted[off - phase:off - phase + TM, :]
    u = acc + bdw_ref[...]
    mu = jnp.mean(u, axis=-1, keepdims=True)
    var = jnp.mean(jnp.square(u - mu), axis=-1, keepdims=True)
    u = _silu((u - mu) * lax.rsqrt(var + LN_EPS) * lg_ref[...] + lb_ref[...])
    out = _mm(u, wo_ref[...]) + bo_ref[...]
    y_ref[...] = x_ref[...] + mod_ref[0][5:6] * out


def _conv_out_call(u, x, mods_l, wdw, bdw, lg, lb, wo, bo):
    per = TM // HALO
    last = N_TOK // HALO - 1
    return pl.pallas_call(
        _conv_out_kernel,
        out_shape=jax.ShapeDtypeStruct((N_TOK, D), F32),
        grid=(N_TOK // TM,),
        in_specs=[_tok(D),
                  pl.BlockSpec((HALO, D), lambda i: (jnp.maximum(i * per - 1, 0), 0)),
                  pl.BlockSpec((HALO, D), lambda i: (jnp.minimum((i + 1) * per, last), 0)),
                  _tok(D), _modspec(), _full((CV_WIDTH, D)), _full((1, D)), _full((1, D)), _full((1, D)),
                  _full((D, D)), _full((1, D))],
        out_specs=_tok(D),
        scratch_shapes=[pltpu.VMEM((TM + 2 * HALO, D), F32)],
        compiler_params=_params(),
        name="conv_out",
    )(u, u, u, x, mods_l, wdw, _row(bdw), _row(lg), _row(lb), wo, _row(bo))


def _diff_proj_kernel(x_ref, mod_ref, g_ref, w_ref, cos_ref, sin_ref, q_ref, kc_ref, kl_ref, vc_ref, vl_ref):
    m = mod_ref[0]
    h = _rms(x_ref[...], g_ref[...]) * (1.0 + m[4:5]) + m[3:4]
    qkv = _mm(h, w_ref[...])
    cos, sin_s = cos_ref[...], sin_ref[...]
    q_ref[...] = _rope(qkv[:, :D], cos, sin_s)
    _store_ctx_lat(kc_ref, kl_ref, _rope(qkv[:, D:2 * D], cos, sin_s))
    _store_ctx_lat(vc_ref, vl_ref, qkv[:, 2 * D:])


def _diff_proj_call(x, mods_l, g, w, cos_t, sin_t):
    return pl.pallas_call(
        _diff_proj_kernel,
        out_shape=(jax.ShapeDtypeStruct((N_TOK, D), F32),) + _ctx_lat_shapes(D) * 2,
        grid=(N_TOK // TM,),
        in_specs=[_tok(D), _modspec(), _full((1, D)), _full(w.shape), _ropespec(), _ropespec()],
        out_specs=(_tok(D),) + _ctx_lat_specs((TM, D)) * 2,
        compiler_params=_params(),
        name="diff_proj",
    )(x, mods_l, _row(g), w, cos_t, sin_t)


def _diff_attn_kernel(q_ref, lam_ref, sg_ref, *refs):
    o_ref = refs[-1]
    k_refs, v_refs = refs[0:-1:2], refs[1:-1:2]
    lv = lam_ref[...]
    lam = (jnp.exp(jnp.sum(lv[0:1] * lv[1:2], axis=-1, keepdims=True))
           - jnp.exp(jnp.sum(lv[2:3] * lv[3:4], axis=-1, keepdims=True)) + DF_LAMBDA_INIT)
    scale = HD ** -0.5
    for h in range(DF_HEADS):
        c1, c2 = (2 * h) * HD, (2 * h + 1) * HD
        es1, l1 = _softmax_parts(q_ref[:, c1:c1 + HD] * scale, [r[:, c1:c1 + HD] for r in k_refs])
        es2, l2 = _softmax_parts(q_ref[:, c2:c2 + HD] * scale, [r[:, c2:c2 + HD] for r in k_refs])
        inv1, inv2 = 1.0 / l1, lam / l2
        o = functools.reduce(jnp.add, [_mm(e1 * inv1 - e2 * inv2, r[:, c1:c1 + 2 * HD])
                                       for e1, e2, r in zip(es1, es2, v_refs)])
        o = _rms(o, sg_ref[...], DF_SUBLN_EPS) * (1.0 - DF_LAMBDA_INIT)
        o_ref[:, c1:c1 + 2 * HD] = o


def _diff_attn_call(q, k, v, lam_rows, subln_g, cache_k=None, cache_v=None):
    latent = cache_k is not None
    grid, q_spec, kv_specs, o_spec = _attn_specs(latent, D)
    parts = [(cache_k, cache_v), (k, v)] if latent else [(k, v)]
    nd = len(grid)
    return pl.pallas_call(
        _diff_attn_kernel,
        out_shape=jax.ShapeDtypeStruct((N_LAT if latent else N_CTX, D), F32),
        grid=grid,
        in_specs=[q_spec, pl.BlockSpec((4, HD), lambda *_: (0, 0)), pl.BlockSpec((1, 2 * HD), lambda *_: (0, 0))]
        + [s for s in kv_specs for _ in range(2)],
        out_specs=o_spec,
        compiler_params=_params(nd),
        name="diff_attn",
    )(q, lam_rows, _row(subln_g), *[a for part in parts for a in part])


def _rwkv_in_kernel(x_ref, xp_ref, xn_ref, mod_ref, g_ref, mix_ref, wr_ref, wk_ref, wv_ref, g1_ref, g2_ref,
                    w1_ref, w2_ref, w0_ref, a1_ref, a2_ref, a0_ref, kk_ref, ka_ref, rk_ref,
                    r_out, v_out, kn_out, g_out, bonus_out, lw_out, kd_out, al_out):
    m = mod_ref[0]
    gn = g_ref[...]

    def norm_mod(z):
        return _rms(z, gn) * (1.0 + m[4:5]) + m[3:4]

    start, end = _seq_edges(pl.program_id(0))
    h = norm_mod(x_ref[...])
    h_prev = jnp.where(start, 0.0, norm_mod(xp_ref[...])[7:8])
    h_next = jnp.where(end, 0.0, norm_mod(xn_ref[...])[0:1])
    row = lax.broadcasted_iota(jnp.int32, (TM, D), 0)
    h_dn = jnp.where(row == 0, h_prev, pltpu.roll(h, 1, 0))
    h_up = jnp.where(row == TM - 1, h_next, pltpu.roll(h, TM - 1, 0))
    xx = 0.5 * (h_dn + h_up) - h
    mix = mix_ref[...]
    xr, xw, xk, xv, xa, xg = (h + xx * mix[i:i + 1] for i in range(6))

    r = _mm(xr, wr_ref[...])
    k = _mm(xk, wk_ref[...])
    v = _mm(xv, wv_ref[...])
    g = _mm(jax.nn.sigmoid(_mm(xg, g1_ref[...])), g2_ref[...])
    wl = w0_ref[...] + _mm(jnp.tanh(_mm(xw, w1_ref[...])), w2_ref[...])
    log_decay = -jnp.exp(-0.5) * jax.nn.sigmoid(wl)
    alpha = jax.nn.sigmoid(a0_ref[...] + _mm(_mm(xa, a1_ref[...]), a2_ref[...]))

    ones = _seg_ones()
    kk = k * kk_ref[...]
    kk = kk * lax.rsqrt(jnp.maximum(_seg_sum(kk * kk, ones), 1e-24))
    r_out[...] = r
    v_out[...] = v
    kn_out[...] = kk
    g_out[...] = g
    kd_sum = jnp.zeros((TM, D), F32)
    for d in range(2):
        al = alpha[:, d * D:(d + 1) * D]
        kd = k * (1.0 + (al - 1.0) * ka_ref[...])
        kd_sum = kd_sum + kd
        lw_out[d] = log_decay[:, d * D:(d + 1) * D]
        kd_out[d] = kd
        al_out[d] = al
    bonus_out[...] = _seg_sum(r * kd_sum * rk_ref[...], ones) * v


def _rwkv_in_call(x, mods_l, g, mix, wr, wk, wv, g1, g2, w1, w2, w0, a1, a2, a0, k_k, k_a, r_k):
    per = TM // 8
    last = N_TOK // 8 - 1
    tok_out = jax.ShapeDtypeStruct((N_TOK, D), F32)
    dir_out = jax.ShapeDtypeStruct((2, N_TOK, D), F32)
    dir_spec = pl.BlockSpec((2, TM, D), lambda i: (0, i, 0))
    return pl.pallas_call(
        _rwkv_in_kernel,
        out_shape=(tok_out,) * 5 + (dir_out,) * 3,
        grid=(N_TOK // TM,),
        in_specs=[_tok(D),
                  pl.BlockSpec((8, D), lambda i: (jnp.maximum(i * per - 1, 0), 0)),
                  pl.BlockSpec((8, D), lambda i: (jnp.minimum((i + 1) * per, last), 0)),
                  _modspec(), _full((1, D)), _full((6, D)),
                  _full((D, D)), _full((D, D)), _full((D, D)), _full((D, GATE_LORA)), _full((GATE_LORA, D)),
                  _full((D, 2 * LORA)), _full((2 * LORA, 2 * D)), _full((1, 2 * D)),
                  _full((D, 2 * LORA)), _full((2 * LORA, 2 * D)), _full((1, 2 * D)),
                  _full((1, D)), _full((1, D)), _full((1, D))],
        out_specs=(_tok(D),) * 5 + (dir_spec,) * 3,
        compiler_params=_params(),
        name="rwkv_in",
    )(x, x, x, mods_l, _row(g), mix, wr, wk, wv, g1, g2, w1, w2, _row(w0), a1, a2, _row(a0),
      _row(k_k), _row(k_a), _row(r_k))


def _block_diag_rows(z):
    zb = z.astype(BF16)
    lane_head = lax.broadcasted_iota(jnp.int32, zb.shape, 1) // HD
    zero = jnp.zeros_like(zb)
    return jnp.concatenate([jnp.where(lane_head == h, zb, zero) for h in range(QUAD // HD)], axis=0)


def _wkv_prep(rev, r_ref, v_ref, kn_ref, lw_ref, kd_ref, al_ref):
    C = WKV_CHUNK
    row = lax.broadcasted_iota(jnp.int32, (C, C), 0)
    col = lax.broadcasted_iota(jnp.int32, (C, C), 1)
    incl = (col >= row) if rev else (col <= row)
    lw = lw_ref[0]
    hi, mid, lo = _split3(lw)
    inc_b = incl.astype(BF16)
    dot = functools.partial(jnp.dot, preferred_element_type=F32)
    g_incl = dot(inc_b, hi) + (dot(inc_b, mid) + dot(inc_b, lo))
    g_tot = jnp.sum(lw, axis=0, keepdims=True)
    kn = kn_ref[...]
    b = kn * al_ref[0]
    kd = kd_ref[0]
    e_neg = jnp.exp(-g_incl)
    e_rem = jnp.exp(g_tot - g_incl)
    return dict(a_t=-kn * jnp.exp(g_incl - lw), r_t=r_ref[...] * jnp.exp(g_incl), b_t=b * e_neg, k_t=kd * e_neg,
                b_h=b * e_rem, k_h=kd * e_rem, w_tot=jnp.exp(g_tot), v=v_ref[...])


def _wkv_kernel(*refs, n_chunks, has_init):
    fwd_refs, bwd_refs, rest = refs[0:6], refs[6:12], refs[12:]
    if has_init:
        s0_ref, yf_ref, yb_ref, sfin_ref, s_ref = rest
    else:
        yf_ref, yb_ref, sfin_ref, s_ref = rest
    c = pl.program_id(1)
    C = WKV_CHUNK

    @pl.when(c == 0)
    def _():
        if has_init:
            for d in range(2):
                for h in range(RW_HEADS):
                    s_ref[d, :, h * HD:(h + 1) * HD] = s0_ref[0, d, h]
        else:
            s_ref[...] = jnp.zeros_like(s_ref)

    row4 = lax.broadcasted_iota(jnp.int32, (C, QUAD), 0)
    col4 = lax.broadcasted_iota(jnp.int32, (C, QUAD), 1) % C
    r2 = lax.broadcasted_iota(jnp.int32, (QUAD, QUAD), 0)
    c2 = lax.broadcasted_iota(jnp.int32, (QUAD, QUAD), 1)
    same_head = (r2 // HD) == (c2 // HD)
    strict4 = (col4 < row4, col4 > row4)
    incl4 = (col4 <= row4, col4 >= row4)

    prep = (_wkv_prep(False, *fwd_refs), _wkv_prep(True, *bwd_refs))
    y_refs = (yf_ref, yb_ref)
    chains = [(d, slice(q * QUAD, (q + 1) * QUAD)) for d in range(2) for q in range(D // QUAD)]

    def rows(name):
        return [prep[d][name][:, sl] for d, sl in chains]

    a_t, r_t, b_t, k_t, b_h, k_h, w_tot, v = (rows(n) for n in ("a_t", "r_t", "b_t", "k_t", "b_h", "k_h", "w_tot", "v"))
    s_in = [s_ref[d, :, sl] for d, sl in chains]
    bd = _block_diag_rows
    dot = functools.partial(jnp.dot, preferred_element_type=F32)
    cat = functools.partial(jnp.concatenate, axis=0)
    ar = [cat([a, r]) for a, r in zip(a_t, r_t)]
    gram_b = [_mm_nt(z, bd(b)) for z, b in zip(ar, b_t)]
    gram_k = [_mm_nt(z, bd(k)) for z, k in zip(ar, k_t)]
    a_ab = [jnp.where(strict4[d], g[:C], 0.0) for (d, _), g in zip(chains, gram_b)]
    a_ak = [jnp.where(strict4[d], g[:C], 0.0) for (d, _), g in zip(chains, gram_k)]
    a_rb = [jnp.where(incl4[d], g[C:], 0.0) for (d, _), g in zip(chains, gram_b)]
    a_rk = [jnp.where(incl4[d], g[C:], 0.0) for (d, _), g in zip(chains, gram_k)]
    H = C // 2
    diag_blk = (row4 // H) == (col4 // H)
    l_d = [jnp.where(diag_blk, l, 0.0) for l in a_ab]
    l_o = [jnp.where(diag_blk, 0.0, l) for l in a_ab]
    rowh = lax.broadcasted_iota(jnp.int32, (H, QUAD), 0)
    laneh = lax.broadcasted_iota(jnp.int32, (H, QUAD), 1)
    eye8 = ((laneh % H) == rowh).astype(F32)

    def bd8(z):
        grp = lax.broadcasted_iota(jnp.int32, z.shape, 1) // H
        zero = jnp.zeros_like(z)
        return cat([jnp.where(grp == g, z, zero) for g in range(QUAD // H)])

    p = [z[:H] + z[H:] for z in l_d]
    x = [eye8 + z for z in p]
    n_steps = 5
    for i in range(n_steps):
        w = [bd8(pq.astype(BF16)) for pq in p]
        if i == 0:
            p = [dot(pq.astype(BF16), wq) for pq, wq in zip(p, w)]
        elif i < n_steps - 1:
            r1 = [dot(cat([pq, xq]).astype(BF16), wq) for pq, xq, wq in zip(p, x, w)]
            p = [a[:H] for a in r1]
            x = [xq + a[H:] for xq, a in zip(x, r1)]
        else:
            x = [xq + dot(xq.astype(BF16), wq) for xq, wq in zip(x, w)]
    top = (laneh % C) < H
    t_d = [cat([jnp.where(top, xq, 0.0), jnp.where(top, 0.0, xq)]) for xq in x]
    z = [_mm(t, bd(lo_)) for t, lo_ in zip(t_d, l_o)]
    x = [t + _mm(zq, bd(t)) for t, zq in zip(t_d, z)]
    vbd = [bd(z) for z in v]
    rv = [_mm(cat([m, n]), vb) for m, n, vb in zip(a_ak, a_rk, vbd)]
    akv = [z[:C] for z in rv]
    wm = [_mm(xq, bd(a)) for xq, a in zip(x, a_t)]
    um = [_mm(xq, bd(z)) for xq, z in zip(x, akv)]
    qm = [r + _mm(m, bd(w)) for r, m, w in zip(r_t, a_rb, wm)]
    y0 = [_mm(m, bd(z)) + z2[C:] for m, z, z2 in zip(a_rb, um, rv)]
    rs = [_mm_nt(cat([w, qq]), bd(sq)) for w, qq, sq in zip(wm, qm, s_in)]
    u = [z[:C] + z2 for z, z2 in zip(rs, um)]
    for (d, sl), z, yy in zip(chains, rs, y0):
        y_refs[d][:, sl] = z[C:] + yy
    for (d, sl), uq, vq, bq, kq, wq, sq in zip(chains, u, v, b_h, k_h, w_tot, s_in):
        upd = _mm_tn(jnp.concatenate([uq, vq], axis=0), jnp.concatenate([bq, kq], axis=0))
        upd = jnp.where(same_head, upd, 0.0)
        upd = (upd[0:HD] + upd[HD:2 * HD]) + (upd[2 * HD:3 * HD] + upd[3 * HD:])
        s_ref[d, :, sl] = sq * wq + upd

    @pl.when(c == n_chunks - 1)
    def _():
        for d in range(2):
            for h in range(RW_HEADS):
                sfin_ref[0, d, h] = s_ref[d, :, h * HD:(h + 1) * HD]


def _wkv_call(r, v, kn, lw, kd, al, n_seq, seq_len, row_off, s0=None):
    C = WKV_CHUNK
    n_chunks = seq_len // C
    off = row_off // C

    def fwd(s, c):
        return s * n_chunks + c

    def bwd(s, c):
        return s * n_chunks + n_chunks - 1 - c

    def specs(blk, d):
        tok = pl.BlockSpec((C, D), lambda s, c: (off + blk(s, c), 0))
        dirtok = pl.BlockSpec((1, C, D), lambda s, c: (d, off + blk(s, c), 0))
        return [tok, tok, tok, dirtok, dirtok, dirtok]

    state = pl.BlockSpec((1, 2, RW_HEADS, HD, HD), lambda s, c: (s, 0, 0, 0, 0))
    has_init = s0 is not None
    in_specs = specs(fwd, 0) + specs(bwd, 1) + ([state] if has_init else [])
    args = [r, v, kn, lw, kd, al] * 2 + ([s0] if has_init else [])
    y_shape = jax.ShapeDtypeStruct((n_seq * seq_len, D), F32)
    return pl.pallas_call(
        functools.partial(_wkv_kernel, n_chunks=n_chunks, has_init=has_init),
        out_shape=(y_shape, y_shape, jax.ShapeDtypeStruct((n_seq, 2, RW_HEADS, HD, HD), F32)),
        grid=(n_seq, n_chunks),
        in_specs=in_specs,
        out_specs=(pl.BlockSpec((C, D), lambda s, c: (fwd(s, c), 0)),
                   pl.BlockSpec((C, D), lambda s, c: (bwd(s, c), 0)), state),
        scratch_shapes=[pltpu.VMEM((2, HD, D), F32)],
        compiler_params=_params(2),
        name="wkv_scan",
    )(*args)


def _rwkv_out_kernel(yfc_ref, yfl_ref, ybc_ref, ybl_ref, bonus_ref, g_ref, x_ref, mod_ref, lg_ref, lb_ref, wo_ref,
                     o_ref):
    ones = _seg_ones()
    y = _ctx_or_lat(yfc_ref, yfl_ref) + _ctx_or_lat(ybc_ref, ybl_ref)
    mu = _seg_sum(y, ones) * (1.0 / HD)
    yc = y - mu
    var = _seg_sum(yc * yc, ones) * (1.0 / HD)
    out = yc * lax.rsqrt(var + RW_GN_EPS) * lg_ref[...] + lb_ref[...] + bonus_ref[...]
    o_ref[...] = x_ref[...] + mod_ref[0][5:6] * _mm(out * g_ref[...], wo_ref[...])


def _rwkv_out_call(y_ctx, y_lat, bonus, g, x, mods_l, lg, lb, wo):
    return pl.pallas_call(
        _rwkv_out_kernel,
        out_shape=jax.ShapeDtypeStruct((N_TOK, D), F32),
        grid=(N_TOK // TM,),
        in_specs=[*_ctx_lat_specs((TM, D)), *_ctx_lat_specs((TM, D)), _tok(D), _tok(D), _tok(D), _modspec(),
                  _full((1, D)), _full((1, D)), _full((D, D))],
        out_specs=_tok(D),
        compiler_params=_params(),
        name="rwkv_out",
    )(y_ctx[0], y_lat[0], y_ctx[1], y_lat[1], bonus, g, x, mods_l, _row(lg), _row(lb), wo)


def _rope_tables():
    t = jnp.arange(DEC_SEQ)
    rowp = (t // GRID_W).astype(F32)
    colp = (t % GRID_W).astype(F32)
    axis_dim = HD // 2
    freqs = ROPE_THETA ** (-jnp.arange(0, axis_dim, 2, dtype=F32) / axis_dim)
    ang = jnp.concatenate([rowp[:, None] * freqs, colp[:, None] * freqs], axis=-1)
    cos = jnp.repeat(jnp.cos(ang), 2, axis=-1)
    sin = jnp.repeat(jnp.sin(ang), 2, axis=-1) * jnp.tile(jnp.array([-1.0, 1.0], F32), HD // 2)
    cos = jnp.concatenate([jnp.ones((TM, HD), F32), cos], axis=0)
    sin = jnp.concatenate([jnp.zeros((TM, HD), F32), sin], axis=0)
    return jnp.tile(cos, (1, LANES // HD)), jnp.tile(sin, (1, LANES // HD))


def _block_diag2(w):
    z = jnp.zeros_like(w[0])
    return jnp.concatenate([jnp.concatenate([w[0], z], axis=1), jnp.concatenate([z, w[1]], axis=1)], axis=0)


def kernel(x_prompt, x_sample, c, c_ctx, cache_k0, cache_v0, cache_k2, cache_v2, state_wkv3,
           norm_g, mod_w, mod_b, ffn_w_in, ffn_w_down, final_norm_g,
           gq_w_qkv, gq_q_norm, gq_k_norm, gq_w_o,
           cv_w_in, cv_b_in, cv_w_dw, cv_b_dw, cv_ln_g, cv_ln_b, cv_w_out, cv_b_out,
           df_w_qkv, df_lambda_q1, df_lambda_k1, df_lambda_q2, df_lambda_k2, df_subln_g, df_w_o,
           rw_mix, rw_w_r, rw_w_k, rw_w_v, rw_w_o, rw_k_k, rw_k_a, rw_r_k, rw_g1, rw_g2,
           rw_ln_g, rw_ln_b, rw_w0, rw_w1, rw_w2, rw_a0, rw_a1, rw_a2):
    bf = lambda w: w.astype(BF16)
    cond8 = jnp.concatenate([c_ctx[None, :], c, jnp.zeros((8 - 1 - DEC_BATCH, D), F32)], axis=0)
    mods = _mods_call(cond8, mod_w, mod_b)
    cos_t, sin_t = _rope_tables()

    def ffn(x, layer, which, final_g=None):
        return _ffn_call(x, mods[layer], norm_g[layer, 2 * which], ffn_w_in, ffn_w_down, layer, which, final_g)

    x = ffn((x_prompt.reshape(N_CTX, D), x_sample.reshape(N_LAT, D)), 0, 0)
    q, kc, kl, vc, vl = _gqa_proj_call(x, mods[0], norm_g[0, 1], bf(gq_w_qkv), gq_q_norm, gq_k_norm, cos_t, sin_t)
    new_k0 = kc.reshape(BATCH, SEQ, GQ_KV, HD)
    new_v0 = vc.reshape(BATCH, SEQ, GQ_KV, HD)
    oc = _gqa_attn_call(q, kc, vc)
    ol = _gqa_attn_call(q, kl, vl, cache_k0.reshape(DEC_BATCH * PAST, GQ_KVW),
                        cache_v0.reshape(DEC_BATCH * PAST, GQ_KVW))
    x = _out_call(oc, ol, x, mods[0], bf(gq_w_o))
    x = ffn(x, 0, 1)

    x = ffn(x, 1, 0)
    u = _conv_in_call(x, mods[1], norm_g[1, 1], bf(cv_w_in), cv_b_in)
    x = _conv_out_call(u, x, mods[1], cv_w_dw, cv_b_dw, cv_ln_g, cv_ln_b, bf(cv_w_out), cv_b_out)
    x = ffn(x, 1, 1)

    x = ffn(x, 2, 0)
    q, kc, kl, vc, vl = _diff_proj_call(x, mods[2], norm_g[2, 1], bf(df_w_qkv), cos_t, sin_t)
    new_k2 = kc.reshape(BATCH, SEQ, DF_HEADS, 2, HD)
    new_v2 = vc.reshape(BATCH, SEQ, DF_HEADS, 2 * HD)
    lam_rows = jnp.stack([df_lambda_q1, df_lambda_k1, df_lambda_q2, df_lambda_k2]).astype(F32)
    oc = _diff_attn_call(q, kc, vc, lam_rows, df_subln_g)
    ol = _diff_attn_call(q, kl, vl, lam_rows, df_subln_g, cache_k2.reshape(DEC_BATCH * PAST, D),
                         cache_v2.reshape(DEC_BATCH * PAST, D))
    x = _out_call(oc, ol, x, mods[2], bf(df_w_o))
    x = ffn(x, 2, 1)

    x = ffn(x, 3, 0)
    w1 = bf(jnp.concatenate([rw_w1[0], rw_w1[1]], axis=1))
    a1 = bf(jnp.concatenate([rw_a1[0], rw_a1[1]], axis=1))
    r, v, kn, g, bonus, lw, kd, al = _rwkv_in_call(
        x, mods[3], norm_g[3, 1], rw_mix, bf(rw_w_r), bf(rw_w_k), bf(rw_w_v), bf(rw_g1), bf(rw_g2),
        w1, bf(_block_diag2(rw_w2)), rw_w0.reshape(-1), a1, bf(_block_diag2(rw_a2)), rw_a0.reshape(-1),
        rw_k_k, rw_k_a, rw_r_k.reshape(-1))
    *y_ctx, new_wkv3 = _wkv_call(r, v, kn, lw, kd, al, BATCH, SEQ, 0)
    *y_lat, _ = _wkv_call(r, v, kn, lw, kd, al, DEC_BATCH, DEC_SEQ, N_CTX, state_wkv3)
    x = _rwkv_out_call(y_ctx, y_lat, bonus, g, x, mods[3], rw_ln_g, rw_ln_b, bf(rw_w_o))
    y_prompt, y_sample = ffn(x, 3, 1, final_norm_g)
    y_prompt = y_prompt.reshape(BATCH, SEQ, D)
    y_sample = y_sample.reshape(DEC_BATCH, DEC_SEQ, D)
    return (y_prompt, y_sample, new_k0, new_v0, new_k2, new_v2, new_wkv3)
```

```python
import functools

import jax
import jax.numpy as jnp
from jax import lax
from jax.experimental import pallas as pl
from jax.experimental.pallas import tpu as pltpu

F32 = jnp.float32
BF16 = jnp.bfloat16

D = 1024
BATCH, SEQ = 32, 256
DEC_BATCH, DEC_SEQ = 2, 2048
PAST = 512
DEPTH = 4
GRID_W = 64
N_MOD = 9
NORM_EPS = 1e-6
LN_EPS = 1e-5
ROPE_THETA = 10000.0
D_FF = 2816
HD = 64
GQ_HEADS, GQ_KV = 16, 4
GQ_GROUP = GQ_HEADS // GQ_KV
GQ_KVW = GQ_KV * HD
CV_WIDTH = 31
CV_PAD = CV_WIDTH // 2
DF_HEADS = 8
DF_LAMBDA_INIT = 0.470713018
DF_SUBLN_EPS = 1e-5
RW_HEADS = 16
RW_GN_EPS = 64e-5
LORA = 64
GATE_LORA = 128

N_CTX = BATCH * SEQ
N_LAT = DEC_BATCH * DEC_SEQ
N_TOK = N_CTX + N_LAT

LANES = 128
TM = 256
TM_FFN = 512
FF_CHUNK = 256
WKV_CHUNK = 64
QUAD = 4 * HD
VMEM_LIMIT = 56 * 1024 * 1024


def _mm(a, b):
    return jnp.dot(a.astype(BF16), b.astype(BF16), preferred_element_type=F32)


def _mm_nt(a, b):
    return lax.dot_general(a.astype(BF16), b.astype(BF16), (((1,), (1,)), ((), ())),
                           preferred_element_type=F32)


def _mm_tn(a, b):
    return lax.dot_general(a.astype(BF16), b.astype(BF16), (((0,), (0,)), ((), ())),
                           preferred_element_type=F32)


def _split3(x):
    hi = x.astype(BF16)
    r1 = x - hi.astype(F32)
    mid = r1.astype(BF16)
    lo = (r1 - mid.astype(F32)).astype(BF16)
    return hi, mid, lo


def _split2(x):
    hi = x.astype(BF16)
    return hi, (x - hi.astype(F32)).astype(BF16)


def _silu(x):
    return x * jax.nn.sigmoid(x)


def _rms(x, g, eps=NORM_EPS):
    return x * lax.rsqrt(jnp.mean(x * x, axis=-1, keepdims=True) + eps) * g


def _seg_ones():
    r = lax.broadcasted_iota(jnp.int32, (LANES, LANES), 0) // HD
    c = lax.broadcasted_iota(jnp.int32, (LANES, LANES), 1) // HD
    return (r == c).astype(BF16)


def _seg_sum(x, ones):
    outs = []
    for g in range(x.shape[1] // LANES):
        hi, lo = _split2(x[:, g * LANES:(g + 1) * LANES])
        dot = functools.partial(jnp.dot, preferred_element_type=F32)
        outs.append(dot(hi, ones) + dot(lo, ones))
    return outs[0] if len(outs) == 1 else jnp.concatenate(outs, axis=1)


def _rope(x, cos, sin_s):
    lane = lax.broadcasted_iota(jnp.int32, (x.shape[0], LANES), 1)
    even = (lane % 2) == 0
    outs = []
    for g in range(x.shape[1] // LANES):
        xg = x[:, g * LANES:(g + 1) * LANES]
        partner = jnp.where(even, pltpu.roll(xg, LANES - 1, 1), pltpu.roll(xg, 1, 1))
        outs.append(xg * cos + partner * sin_s)
    return outs[0] if len(outs) == 1 else jnp.concatenate(outs, axis=1)


def _cond_row(i, tm):
    nt_ctx = N_CTX // tm
    return jnp.where(i < nt_ctx, 0, 1 + (i - nt_ctx) // (DEC_SEQ // tm))


def _tok(width, tm=TM):
    return pl.BlockSpec((tm, width), lambda i: (i, 0))


def _full(shape):
    nd = len(shape)
    return pl.BlockSpec(tuple(shape), lambda i: (0,) * nd, pipeline_mode=pl.Buffered(1))


def _modspec(tm=TM):
    return pl.BlockSpec((1, N_MOD, D), lambda i: (_cond_row(i, tm), 0, 0))


def _ropespec():
    nt_ctx = N_CTX // TM
    per_seq = DEC_SEQ // TM
    return pl.BlockSpec((TM, LANES), lambda i: (jnp.where(i < nt_ctx, 0, 1 + (i - nt_ctx) % per_seq), 0))


def _params(n_axes=1):
    return pltpu.CompilerParams(dimension_semantics=("arbitrary",) * n_axes, vmem_limit_bytes=VMEM_LIMIT)


def _row(v):
    return v.reshape(1, -1).astype(F32)


def _ctx_lat_specs(block, tm=TM):
    nt_ctx = N_CTX // tm
    lead = (0,) * (len(block) - 2)
    return (pl.BlockSpec(block, lambda i: lead + (jnp.minimum(i, nt_ctx - 1), 0)),
            pl.BlockSpec(block, lambda i: lead + (jnp.maximum(i - nt_ctx, 0), 0)))


def _ctx_lat_shapes(width, lat_dtype=F32):
    return (jax.ShapeDtypeStruct((N_CTX, width), F32), jax.ShapeDtypeStruct((N_LAT, width), lat_dtype))


def _ctx_or_lat(c_ref, l_ref, tm=TM):
    return jnp.where(pl.program_id(0) < N_CTX // tm, c_ref[...], l_ref[...])


def _store_ctx_lat(c_ref, l_ref, val, tm=TM):
    l_ref[...] = val.astype(l_ref.dtype)

    @pl.when(pl.program_id(0) < N_CTX // tm)
    def _():
        c_ref[...] = val.astype(c_ref.dtype)


def _mods_kernel(c_ref, w_ref, b_ref, o_ref):
    o_ref[0] = _mm(_silu(c_ref[...]), w_ref[0]) + b_ref[0]


def _mods_call(cond8, mod_w, mod_b):
    out = pl.pallas_call(
        _mods_kernel,
        out_shape=jax.ShapeDtypeStruct((DEPTH, 8, N_MOD * D), F32),
        grid=(DEPTH, N_MOD),
        in_specs=[pl.BlockSpec((8, D), lambda l, j: (0, 0)),
                  pl.BlockSpec((1, D, D), lambda l, j: (l, 0, j)),
                  pl.BlockSpec((1, 1, D), lambda l, j: (l, 0, j))],
        out_specs=pl.BlockSpec((1, 8, D), lambda l, j: (l, 0, j)),
        compiler_params=_params(2),
        name="adaln_mods",
    )(cond8, mod_w, mod_b.reshape(DEPTH, 1, N_MOD * D))
    return out.reshape(DEPTH, 8, N_MOD, D)


N_STAGE = 11
IN_COLS = 2 * D_FF // N_STAGE
DOWN_ROWS = D_FF // N_STAGE


def _stage_weights(win_hbm, wd_hbm, win_ref, wd_ref, in_buf, dn_buf, sem, layer, which):
    def in_copy(c, slot):
        return pltpu.make_async_copy(win_hbm.at[layer, which, :, pl.ds(c * IN_COLS, IN_COLS)],
                                     in_buf.at[slot], sem.at[0, slot])

    def dn_copy(c, slot):
        return pltpu.make_async_copy(wd_hbm.at[layer, which, pl.ds(c * DOWN_ROWS, DOWN_ROWS), :],
                                     dn_buf.at[slot], sem.at[1, slot])

    in_copy(0, 0).start()
    dn_copy(0, 0).start()
    for c in range(N_STAGE):
        slot = c % 2
        if c + 1 < N_STAGE:
            in_copy(c + 1, 1 - slot).start()
            dn_copy(c + 1, 1 - slot).start()
        in_copy(c, slot).wait()
        win_ref[:, c * IN_COLS:(c + 1) * IN_COLS] = in_buf[slot].astype(BF16)
        dn_copy(c, slot).wait()
        wd_ref[c * DOWN_ROWS:(c + 1) * DOWN_ROWS, :] = dn_buf[slot].astype(BF16)


def _ffn_kernel(*refs, row0, layer, which, first, final):
    refs = list(refs)
    x = _ctx_or_lat(refs.pop(0), refs.pop(0), TM_FFN) if first else refs.pop(0)[...]
    mod_ref, g_ref, win_hbm, wd_hbm = refs[:4]
    act_ref, win_ref, wd_ref, in_buf, dn_buf, sem = refs[-6:]

    @pl.when(pl.program_id(0) == 0)
    def _():
        _stage_weights(win_hbm, wd_hbm, win_ref, wd_ref, in_buf, dn_buf, sem, layer, which)

    m = mod_ref[0]
    h = _rms(x, g_ref[...]) * (1.0 + m[row0 + 1:row0 + 2]) + m[row0:row0 + 1]
    hb = h.astype(BF16)
    for c in range(D_FF // FF_CHUNK):
        lo = c * FF_CHUNK
        gate = jnp.dot(hb, win_ref[:, lo:lo + FF_CHUNK], preferred_element_type=F32)
        up = jnp.dot(hb, win_ref[:, D_FF + lo:D_FF + lo + FF_CHUNK], preferred_element_type=F32)
        act_ref[:, lo:lo + FF_CHUNK] = (_silu(gate) * up).astype(BF16)
    out = jnp.dot(act_ref[...], wd_ref[...], preferred_element_type=F32)
    y = x + (0.5 * m[row0 + 2:row0 + 3]) * out
    if final:
        fg_ref, oc_ref, ol_ref = refs[4:7]
        _store_ctx_lat(oc_ref, ol_ref, _rms(y, fg_ref[...]), TM_FFN)
    else:
        refs[4][...] = y


def _ffn_call(x, mods_l, g, w_in, w_down, layer, which, final_g=None):
    first = isinstance(x, tuple)
    final = final_g is not None
    hbm = pl.BlockSpec(memory_space=pl.ANY)
    in_specs = (list(_ctx_lat_specs((TM_FFN, D), TM_FFN)) if first else [_tok(D, TM_FFN)]) + [
        _modspec(TM_FFN), _full((1, D)), hbm, hbm]
    args = (list(x) if first else [x]) + [mods_l, _row(g), w_in, w_down]
    if final:
        in_specs.append(_full((1, D)))
        args.append(_row(final_g))
        out_shape = (jax.ShapeDtypeStruct((N_CTX, D), F32), jax.ShapeDtypeStruct((N_LAT, D), F32))
        out_specs = _ctx_lat_specs((TM_FFN, D), TM_FFN)
    else:
        out_shape = jax.ShapeDtypeStruct((N_TOK, D), F32)
        out_specs = _tok(D, TM_FFN)
    return pl.pallas_call(
        functools.partial(_ffn_kernel, row0=6 * which, layer=layer, which=which, first=first, final=final),
        out_shape=out_shape,
        grid=(N_TOK // TM_FFN,),
        in_specs=in_specs,
        out_specs=out_specs,
        scratch_shapes=[pltpu.VMEM((TM_FFN, D_FF), BF16), pltpu.VMEM((D, 2 * D_FF), BF16), pltpu.VMEM((D_FF, D), BF16),
                        pltpu.VMEM((2, D, IN_COLS), F32), pltpu.VMEM((2, DOWN_ROWS, D), F32),
                        pltpu.SemaphoreType.DMA((2, 2))],
        compiler_params=_params(),
        name="ffn_final" if final else "ffn",
    )(*args)


def _out_kernel(oc_ref, ol_ref, x_ref, mod_ref, w_ref, y_ref):
    out = _mm(_ctx_or_lat(oc_ref, ol_ref), w_ref[...])
    y_ref[...] = x_ref[...] + mod_ref[0][5:6] * out


def _out_call(oc, ol, x, mods_l, w):
    return pl.pallas_call(
        _out_kernel,
        out_shape=jax.ShapeDtypeStruct((N_TOK, D), F32),
        grid=(N_TOK // TM,),
        in_specs=[*_ctx_lat_specs((TM, oc.shape[1])), _tok(D), _modspec(), _full(w.shape)],
        out_specs=_tok(D),
        compiler_params=_params(),
        name="out_proj",
    )(oc, ol, x, mods_l, w)


def _gqa_proj_kernel(x_ref, mod_ref, g_ref, w_ref, qn_ref, kn_ref, cos_ref, sin_ref,
                     q_ref, kc_ref, kl_ref, vc_ref, vl_ref):
    m = mod_ref[0]
    h = _rms(x_ref[...], g_ref[...]) * (1.0 + m[4:5]) + m[3:4]
    qkv = _mm(h, w_ref[...])
    ones = _seg_ones()
    cos, sin_s = cos_ref[...], sin_ref[...]
    q = qkv[:, :D]
    k = qkv[:, D:D + GQ_KVW]
    q = q * lax.rsqrt(_seg_sum(q * q, ones) * (1.0 / HD) + NORM_EPS) * qn_ref[...]
    k = k * lax.rsqrt(_seg_sum(k * k, ones) * (1.0 / HD) + NORM_EPS) * kn_ref[...]
    q_ref[...] = _rope(q, cos, sin_s).astype(BF16)
    _store_ctx_lat(kc_ref, kl_ref, _rope(k, cos, sin_s))
    _store_ctx_lat(vc_ref, vl_ref, qkv[:, D + GQ_KVW:])


def _gqa_proj_call(x, mods_l, g, w, qn, kn, cos_t, sin_t):
    return pl.pallas_call(
        _gqa_proj_kernel,
        out_shape=(jax.ShapeDtypeStruct((N_TOK, D), BF16),) + _ctx_lat_shapes(GQ_KVW, BF16) * 2,
        grid=(N_TOK // TM,),
        in_specs=[_tok(D), _modspec(), _full((1, D)), _full(w.shape), _full((1, D)), _full((1, GQ_KVW)),
                  _ropespec(), _ropespec()],
        out_specs=(_tok(D),) + _ctx_lat_specs((TM, GQ_KVW)) * 2,
        compiler_params=_params(),
        name="gqa_proj",
    )(x, mods_l, _row(g), w, _row(jnp.tile(qn, GQ_HEADS)), _row(jnp.tile(kn, GQ_KV)), cos_t, sin_t)


def _softmax_parts(q, ks):
    ss = [_mm_nt(q, k) for k in ks]
    mx = functools.reduce(jnp.maximum, [jnp.max(s, axis=-1, keepdims=True) for s in ss])
    es = [jnp.exp(s - mx) for s in ss]
    return es, functools.reduce(jnp.add, [jnp.sum(e, axis=-1, keepdims=True) for e in es])


def _attn_specs(latent, kv_width):
    if not latent:
        tok = pl.BlockSpec((SEQ, D), lambda i: (i, 0))
        kv = pl.BlockSpec((SEQ, kv_width), lambda i: (i, 0))
        return (BATCH,), tok, [kv], tok
    per = DEC_SEQ // TM
    q = pl.BlockSpec((TM, D), lambda b, j: (N_CTX // TM + b * per + j, 0))
    cache = pl.BlockSpec((PAST, kv_width), lambda b, j: (b, 0))
    own = pl.BlockSpec((DEC_SEQ, kv_width), lambda b, j: (b, 0))
    out = pl.BlockSpec((TM, D), lambda b, j: (b * per + j, 0))
    return (DEC_BATCH, per), q, [cache, own], out


def _gqa_attn_kernel(q_ref, *refs):
    o_ref = refs[-1]
    k_refs, v_refs = refs[0:-1:2], refs[1:-1:2]
    scale = HD ** -0.5
    for kv in range(GQ_KV):
        cols = slice(kv * HD, (kv + 1) * HD)
        ks = [r[:, cols] for r in k_refs]
        vs = [r[:, cols].astype(BF16) for r in v_refs]
        for g in range(GQ_GROUP):
            h = kv * GQ_GROUP + g
            es, l = _softmax_parts(q_ref[:, h * HD:(h + 1) * HD] * scale, ks)
            pv = functools.reduce(jnp.add, [_mm(e, v) for e, v in zip(es, vs)])
            o_ref[:, h * HD:(h + 1) * HD] = (pv / l).astype(BF16)


def _gqa_attn_call(q, k, v, cache_k=None, cache_v=None):
    latent = cache_k is not None
    grid, q_spec, kv_specs, o_spec = _attn_specs(latent, GQ_KVW)
    parts = [(cache_k, cache_v), (k, v)] if latent else [(k, v)]
    return pl.pallas_call(
        _gqa_attn_kernel,
        out_shape=jax.ShapeDtypeStruct((N_LAT if latent else N_CTX, D), BF16),
        grid=grid,
        in_specs=[q_spec] + [s for s in kv_specs for _ in range(2)],
        out_specs=o_spec,
        compiler_params=_params(len(grid)),
        name="gqa_attn",
    )(q, *[a for part in parts for a in part])


def _conv_in_kernel(x_ref, mod_ref, g_ref, w_ref, b_ref, u_ref):
    m = mod_ref[0]
    h = _rms(x_ref[...], g_ref[...]) * (1.0 + m[4:5]) + m[3:4]
    u = _mm(h, w_ref[...]) + b_ref[...]
    u_ref[...] = u[:, :D] * jax.nn.sigmoid(u[:, D:])


def _conv_in_call(x, mods_l, g, w, b):
    return pl.pallas_call(
        _conv_in_kernel,
        out_shape=jax.ShapeDtypeStruct((N_TOK, D), F32),
        grid=(N_TOK // TM,),
        in_specs=[_tok(D), _modspec(), _full((1, D)), _full(w.shape), _full((1, 2 * D))],
        out_specs=_tok(D),
        compiler_params=_params(),
        name="conv_in",
    )(x, mods_l, _row(g), w, _row(b))


HALO = 16


def _seq_edges(i):
    nt_ctx = N_CTX // TM
    per = jnp.where(i < nt_ctx, SEQ // TM, DEC_SEQ // TM)
    j = jnp.where(i < nt_ctx, i, i - nt_ctx) % per
    return j == 0, j == per - 1


def _conv_out_kernel(u_ref, up_ref, un_ref, x_ref, mod_ref, wdw_ref, bdw_ref, lg_ref, lb_ref, wo_ref, bo_ref,
                     y_ref, pad_ref):
    start, end = _seq_edges(pl.program_id(0))
    pad_ref[0:HALO, :] = jnp.where(start, 0.0, up_ref[...])
    pad_ref[HALO:HALO + TM, :] = u_ref[...]
    pad_ref[HALO + TM:, :] = jnp.where(end, 0.0, un_ref[...])
    wdw = wdw_ref[...]
    pad = pad_ref[...]
    n_pad = TM + 2 * HALO
    acc = jnp.zeros((TM, D), F32)
    for phase in range(8):
        shifted = pad if phase == 0 else pltpu.roll(pad, n_pad - phase, 0)
        for j in range(CV_WIDTH):
            off = HALO - CV_PAD + j
            if off % 8 == phase:
                acc = acc + wdw[j:j + 1] * shifted[off - phase:off - phase + TM, :]
    u = acc + bdw_ref[...]
    mu = jnp.mean(u, axis=-1, keepdims=True)
    var = jnp.mean(jnp.square(u - mu), axis=-1, keepdims=True)
    u = _silu((u - mu) * lax.rsqrt(var + LN_EPS) * lg_ref[...] + lb_ref[...])
    out = _mm(u, wo_ref[...]) + bo_ref[...]
    y_ref[...] = x_ref[...] + mod_ref[0][5:6] * out


def _conv_out_call(u, x, mods_l, wdw, bdw, lg, lb, wo, bo):
    per = TM // HALO
    last = N_TOK // HALO - 1
    return pl.pallas_call(
        _conv_out_kernel,
        out_shape=jax.ShapeDtypeStruct((N_TOK, D), F32),
        grid=(N_TOK // TM,),
        in_specs=[_tok(D),
                  pl.BlockSpec((HALO, D), lambda i: (jnp.maximum(i * per - 1, 0), 0)),
                  pl.BlockSpec((HALO, D), lambda i: (jnp.minimum((i + 1) * per, last), 0)),
                  _tok(D), _modspec(), _full((CV_WIDTH, D)), _full((1, D)), _full((1, D)), _full((1, D)),
                  _full((D, D)), _full((1, D))],
        out_specs=_tok(D),
        scratch_shapes=[pltpu.VMEM((TM + 2 * HALO, D), F32)],
        compiler_params=_params(),
        name="conv_out",
    )(u, u, u, x, mods_l, wdw, _row(bdw), _row(lg), _row(lb), wo, _row(bo))


def _diff_proj_kernel(x_ref, mod_ref, g_ref, w_ref, cos_ref, sin_ref, q_ref, kc_ref, kl_ref, vc_ref, vl_ref):
    m = mod_ref[0]
    h = _rms(x_ref[...], g_ref[...]) * (1.0 + m[4:5]) + m[3:4]
    qkv = _mm(h, w_ref[...])
    cos, sin_s = cos_ref[...], sin_ref[...]
    q_ref[...] = _rope(qkv[:, :D], cos, sin_s).astype(BF16)
    _store_ctx_lat(kc_ref, kl_ref, _rope(qkv[:, D:2 * D], cos, sin_s))
    _store_ctx_lat(vc_ref, vl_ref, qkv[:, 2 * D:])


def _diff_proj_call(x, mods_l, g, w, cos_t, sin_t):
    return pl.pallas_call(
        _diff_proj_kernel,
        out_shape=(jax.ShapeDtypeStruct((N_TOK, D), BF16),) + _ctx_lat_shapes(D, BF16) * 2,
        grid=(N_TOK // TM,),
        in_specs=[_tok(D), _modspec(), _full((1, D)), _full(w.shape), _ropespec(), _ropespec()],
        out_specs=(_tok(D),) + _ctx_lat_specs((TM, D)) * 2,
        compiler_params=_params(),
        name="diff_proj",
    )(x, mods_l, _row(g), w, cos_t, sin_t)


def _diff_attn_kernel(q_ref, lam_ref, sg_ref, *refs):
    o_ref = refs[-1]
    k_refs, v_refs = refs[0:-1:2], refs[1:-1:2]
    lv = lam_ref[...]
    lam = (jnp.exp(jnp.sum(lv[0:1] * lv[1:2], axis=-1, keepdims=True))
           - jnp.exp(jnp.sum(lv[2:3] * lv[3:4], axis=-1, keepdims=True)) + DF_LAMBDA_INIT)
    scale = HD ** -0.5
    for h in range(DF_HEADS):
        c1, c2 = (2 * h) * HD, (2 * h + 1) * HD
        es1, l1 = _softmax_parts(q_ref[:, c1:c1 + HD] * scale, [r[:, c1:c1 + HD] for r in k_refs])
        es2, l2 = _softmax_parts(q_ref[:, c2:c2 + HD] * scale, [r[:, c2:c2 + HD] for r in k_refs])
        inv1, inv2 = 1.0 / l1, lam / l2
        o = functools.reduce(jnp.add, [_mm(e1 * inv1 - e2 * inv2, r[:, c1:c1 + 2 * HD])
                                       for e1, e2, r in zip(es1, es2, v_refs)])
        o = _rms(o, sg_ref[...], DF_SUBLN_EPS) * (1.0 - DF_LAMBDA_INIT)
        o_ref[:, c1:c1 + 2 * HD] = o.astype(BF16)


def _diff_attn_call(q, k, v, lam_rows, subln_g, cache_k=None, cache_v=None):
    latent = cache_k is not None
    grid, q_spec, kv_specs, o_spec = _attn_specs(latent, D)
    parts = [(cache_k, cache_v), (k, v)] if latent else [(k, v)]
    nd = len(grid)
    return pl.pallas_call(
        _diff_attn_kernel,
        out_shape=jax.ShapeDtypeStruct((N_LAT if latent else N_CTX, D), BF16),
        grid=grid,
        in_specs=[q_spec, pl.BlockSpec((4, HD), lambda *_: (0, 0)), pl.BlockSpec((1, 2 * HD), lambda *_: (0, 0))]
        + [s for s in kv_specs for _ in range(2)],
        out_specs=o_spec,
        compiler_params=_params(nd),
        name="diff_attn",
    )(q, lam_rows, _row(subln_g), *[a for part in parts for a in part])


def _rwkv_in_kernel(x_ref, xp_ref, xn_ref, mod_ref, g_ref, mix_ref, wr_ref, wk_ref, wv_ref, g1_ref, g2_ref,
                    w1_ref, w2_ref, w0_ref, a1_ref, a2_ref, a0_ref, kk_ref, ka_ref, rk_ref,
                    r_out, v_out, kn_out, g_out, bonus_out, lw_out, kd_out, al_out):
    m = mod_ref[0]
    gn = g_ref[...]

    def norm_mod(z):
        return _rms(z, gn) * (1.0 + m[4:5]) + m[3:4]

    start, end = _seq_edges(pl.program_id(0))
    h = norm_mod(x_ref[...])
    h_prev = jnp.where(start, 0.0, norm_mod(xp_ref[...])[7:8])
    h_next = jnp.where(end, 0.0, norm_mod(xn_ref[...])[0:1])
    row = lax.broadcasted_iota(jnp.int32, (TM, D), 0)
    h_dn = jnp.where(row == 0, h_prev, pltpu.roll(h, 1, 0))
    h_up = jnp.where(row == TM - 1, h_next, pltpu.roll(h, TM - 1, 0))
    xx = 0.5 * (h_dn + h_up) - h
    mix = mix_ref[...]
    xr, xw, xk, xv, xa, xg = (h + xx * mix[i:i + 1] for i in range(6))

    r = _mm(xr, wr_ref[...])
    k = _mm(xk, wk_ref[...])
    v = _mm(xv, wv_ref[...])
    g = _mm(jax.nn.sigmoid(_mm(xg, g1_ref[...])), g2_ref[...])
    wl = w0_ref[...] + _mm(jnp.tanh(_mm(xw, w1_ref[...])), w2_ref[...])
    log_decay = -jnp.exp(-0.5) * jax.nn.sigmoid(wl)
    alpha = jax.nn.sigmoid(a0_ref[...] + _mm(_mm(xa, a1_ref[...]), a2_ref[...]))

    ones = _seg_ones()
    kk = k * kk_ref[...]
    kk = kk * lax.rsqrt(jnp.maximum(_seg_sum(kk * kk, ones), 1e-24))
    r_out[...] = r
    v_out[...] = v
    kn_out[...] = kk
    g_out[...] = g
    kd_sum = jnp.zeros((TM, D), F32)
    for d in range(2):
        al = alpha[:, d * D:(d + 1) * D]
        kd = k * (1.0 + (al - 1.0) * ka_ref[...])
        kd_sum = kd_sum + kd
        lw_out[d] = log_decay[:, d * D:(d + 1) * D]
        kd_out[d] = kd
        al_out[d] = al
    bonus_out[...] = _seg_sum(r * kd_sum * rk_ref[...], ones) * v


def _rwkv_in_call(x, mods_l, g, mix, wr, wk, wv, g1, g2, w1, w2, w0, a1, a2, a0, k_k, k_a, r_k):
    per = TM // 8
    last = N_TOK // 8 - 1
    tok_out = jax.ShapeDtypeStruct((N_TOK, D), F32)
    dir_out = jax.ShapeDtypeStruct((2, N_TOK, D), F32)
    dir_spec = pl.BlockSpec((2, TM, D), lambda i: (0, i, 0))
    return pl.pallas_call(
        _rwkv_in_kernel,
        out_shape=(tok_out,) * 5 + (dir_out,) * 3,
        grid=(N_TOK // TM,),
        in_specs=[_tok(D),
                  pl.BlockSpec((8, D), lambda i: (jnp.maximum(i * per - 1, 0), 0)),
                  pl.BlockSpec((8, D), lambda i: (jnp.minimum((i + 1) * per, last), 0)),
                  _modspec(), _full((1, D)), _full((6, D)),
                  _full((D, D)), _full((D, D)), _full((D, D)), _full((D, GATE_LORA)), _full((GATE_LORA, D)),
                  _full((D, 2 * LORA)), _full((2 * LORA, 2 * D)), _full((1, 2 * D)),
                  _full((D, 2 * LORA)), _full((2 * LORA, 2 * D)), _full((1, 2 * D)),
                  _full((1, D)), _full((1, D)), _full((1, D))],
        out_specs=(_tok(D),) * 5 + (dir_spec,) * 3,
        compiler_params=_params(),
        name="rwkv_in",
    )(x, x, x, mods_l, _row(g), mix, wr, wk, wv, g1, g2, w1, w2, _row(w0), a1, a2, _row(a0),
      _row(k_k), _row(k_a), _row(r_k))


def _block_diag_rows(z):
    zb = z.astype(BF16)
    lane_head = lax.broadcasted_iota(jnp.int32, zb.shape, 1) // HD
    zero = jnp.zeros_like(zb)
    return jnp.concatenate([jnp.where(lane_head == h, zb, zero) for h in range(QUAD // HD)], axis=0)


def _wkv_prep(rev, r_ref, v_ref, kn_ref, lw_ref, kd_ref, al_ref):
    C = WKV_CHUNK
    row = lax.broadcasted_iota(jnp.int32, (C, C), 0)
    col = lax.broadcasted_iota(jnp.int32, (C, C), 1)
    incl = (col >= row) if rev else (col <= row)
    lw = lw_ref[0]
    hi, mid, lo = _split3(lw)
    inc_b = incl.astype(BF16)
    dot = functools.partial(jnp.dot, preferred_element_type=F32)
    g_incl = dot(inc_b, hi) + (dot(inc_b, mid) + dot(inc_b, lo))
    g_tot = jnp.sum(lw, axis=0, keepdims=True)
    kn = kn_ref[...]
    b = kn * al_ref[0]
    kd = kd_ref[0]
    e_neg = jnp.exp(-g_incl)
    e_rem = jnp.exp(g_tot - g_incl)
    return dict(a_t=-kn * jnp.exp(g_incl - lw), r_t=r_ref[...] * jnp.exp(g_incl), b_t=b * e_neg, k_t=kd * e_neg,
                b_h=b * e_rem, k_h=kd * e_rem, w_tot=jnp.exp(g_tot), v=v_ref[...])


def _wkv_kernel(*refs, n_chunks, has_init):
    fwd_refs, bwd_refs, rest = refs[0:6], refs[6:12], refs[12:]
    if has_init:
        s0_ref, yf_ref, yb_ref, sfin_ref, s_ref = rest
    else:
        yf_ref, yb_ref, sfin_ref, s_ref = rest
    c = pl.program_id(1)
    C = WKV_CHUNK

    @pl.when(c == 0)
    def _():
        if has_init:
            for d in range(2):
                for h in range(RW_HEADS):
                    s_ref[d, :, h * HD:(h + 1) * HD] = s0_ref[0, d, h]
        else:
            s_ref[...] = jnp.zeros_like(s_ref)

    row4 = lax.broadcasted_iota(jnp.int32, (C, QUAD), 0)
    col4 = lax.broadcasted_iota(jnp.int32, (C, QUAD), 1) % C
    r2 = lax.broadcasted_iota(jnp.int32, (QUAD, QUAD), 0)
    c2 = lax.broadcasted_iota(jnp.int32, (QUAD, QUAD), 1)
    same_head = (r2 // HD) == (c2 // HD)
    strict4 = (col4 < row4, col4 > row4)
    incl4 = (col4 <= row4, col4 >= row4)

    prep = (_wkv_prep(False, *fwd_refs), _wkv_prep(True, *bwd_refs))
    y_refs = (yf_ref, yb_ref)
    chains = [(d, slice(q * QUAD, (q + 1) * QUAD)) for d in range(2) for q in range(D // QUAD)]

    def rows(name):
        return [prep[d][name][:, sl] for d, sl in chains]

    a_t, r_t, b_t, k_t, b_h, k_h, w_tot, v = (rows(n) for n in ("a_t", "r_t", "b_t", "k_t", "b_h", "k_h", "w_tot", "v"))
    s_in = [s_ref[d, :, sl] for d, sl in chains]
    bd = _block_diag_rows
    dot = functools.partial(jnp.dot, preferred_element_type=F32)
    cat = functools.partial(jnp.concatenate, axis=0)
    ar = [cat([a, r]) for a, r in zip(a_t, r_t)]
    gram_b = [_mm_nt(z, bd(b)) for z, b in zip(ar, b_t)]
    gram_k = [_mm_nt(z, bd(k)) for z, k in zip(ar, k_t)]
    a_ab = [jnp.where(strict4[d], g[:C], 0.0) for (d, _), g in zip(chains, gram_b)]
    a_ak = [jnp.where(strict4[d], g[:C], 0.0) for (d, _), g in zip(chains, gram_k)]
    a_rb = [jnp.where(incl4[d], g[C:], 0.0) for (d, _), g in zip(chains, gram_b)]
    a_rk = [jnp.where(incl4[d], g[C:], 0.0) for (d, _), g in zip(chains, gram_k)]
    H = C // 2
    diag_blk = (row4 // H) == (col4 // H)
    l_d = [jnp.where(diag_blk, l, 0.0) for l in a_ab]
    l_o = [jnp.where(diag_blk, 0.0, l) for l in a_ab]
    rowh = lax.broadcasted_iota(jnp.int32, (H, QUAD), 0)
    laneh = lax.broadcasted_iota(jnp.int32, (H, QUAD), 1)
    eye8 = ((laneh % H) == rowh).astype(F32)

    def bd8(z):
        grp = lax.broadcasted_iota(jnp.int32, z.shape, 1) // H
        zero = jnp.zeros_like(z)
        return cat([jnp.where(grp == g, z, zero) for g in range(QUAD // H)])

    p = [z[:H] + z[H:] for z in l_d]
    x = [eye8 + z for z in p]
    n_steps = 5
    for i in range(n_steps):
        w = [bd8(pq.astype(BF16)) for pq in p]
        if i == 0:
            p = [dot(pq.astype(BF16), wq) for pq, wq in zip(p, w)]
        elif i < n_steps - 1:
            r1 = [dot(cat([pq, xq]).astype(BF16), wq) for pq, xq, wq in zip(p, x, w)]
            p = [a[:H] for a in r1]
            x = [xq + a[H:] for xq, a in zip(x, r1)]
        else:
            x = [xq + dot(xq.astype(BF16), wq) for xq, wq in zip(x, w)]
    top = (laneh % C) < H
    t_d = [cat([jnp.where(top, xq, 0.0), jnp.where(top, 0.0, xq)]) for xq in x]
    z = [_mm(t, bd(lo_)) for t, lo_ in zip(t_d, l_o)]
    x = [t + _mm(zq, bd(t)) for t, zq in zip(t_d, z)]
    vbd = [bd(z) for z in v]
    rv = [_mm(cat([m, n]), vb) for m, n, vb in zip(a_ak, a_rk, vbd)]
    akv = [z[:C] for z in rv]
    wm = [_mm(xq, bd(a)) for xq, a in zip(x, a_t)]
    um = [_mm(xq, bd(z)) for xq, z in zip(x, akv)]
    qm = [r + _mm(m, bd(w)) for r, m, w in zip(r_t, a_rb, wm)]
    y0 = [_mm(m, bd(z)) + z2[C:] for m, z, z2 in zip(a_rb, um, rv)]
    rs = [_mm_nt(cat([w, qq]), bd(sq)) for w, qq, sq in zip(wm, qm, s_in)]
    u = [z[:C] + z2 for z, z2 in zip(rs, um)]
    for (d, sl), z, yy in zip(chains, rs, y0):
        y_refs[d][:, sl] = z[C:] + yy
    for (d, sl), uq, vq, bq, kq, wq, sq in zip(chains, u, v, b_h, k_h, w_tot, s_in):
        upd = _mm_tn(jnp.concatenate([uq, vq], axis=0), jnp.concatenate([bq, kq], axis=0))
        upd = jnp.where(same_head, upd, 0.0)
        upd = (upd[0:HD] + upd[HD:2 * HD]) + (upd[2 * HD:3 * HD] + upd[3 * HD:])
        s_ref[d, :, sl] = sq * wq + upd

    @pl.when(c == n_chunks - 1)
    def _():
        for d in range(2):
            for h in range(RW_HEADS):
                sfin_ref[0, d, h] = s_ref[d, :, h * HD:(h + 1) * HD]


def _wkv_call(r, v, kn, lw, kd, al, n_seq, seq_len, row_off, s0=None):
    C = WKV_CHUNK
    n_chunks = seq_len // C
    off = row_off // C

    def fwd(s, c):
        return s * n_chunks + c

    def bwd(s, c):
        return s * n_chunks + n_chunks - 1 - c

    def specs(blk, d):
        tok = pl.BlockSpec((C, D), lambda s, c: (off + blk(s, c), 0))
        dirtok = pl.BlockSpec((1, C, D), lambda s, c: (d, off + blk(s, c), 0))
        return [tok, tok, tok, dirtok, dirtok, dirtok]

    state = pl.BlockSpec((1, 2, RW_HEADS, HD, HD), lambda s, c: (s, 0, 0, 0, 0))
    has_init = s0 is not None
    in_specs = specs(fwd, 0) + specs(bwd, 1) + ([state] if has_init else [])
    args = [r, v, kn, lw, kd, al] * 2 + ([s0] if has_init else [])
    y_shape = jax.ShapeDtypeStruct((n_seq * seq_len, D), F32)
    return pl.pallas_call(
        functools.partial(_wkv_kernel, n_chunks=n_chunks, has_init=has_init),
        out_shape=(y_shape, y_shape, jax.ShapeDtypeStruct((n_seq, 2, RW_HEADS, HD, HD), F32)),
        grid=(n_seq, n_chunks),
        in_specs=in_specs,
        out_specs=(pl.BlockSpec((C, D), lambda s, c: (fwd(s, c), 0)),
                   pl.BlockSpec((C, D), lambda s, c: (bwd(s, c), 0)), state),
        scratch_shapes=[pltpu.VMEM((2, HD, D), F32)],
        compiler_params=_params(2),
        name="wkv_scan",
    )(*args)


def _rwkv_out_kernel(yfc_ref, yfl_ref, ybc_ref, ybl_ref, bonus_ref, g_ref, x_ref, mod_ref, lg_ref, lb_ref, wo_ref,
                     o_ref):
    ones = _seg_ones()
    y = _ctx_or_lat(yfc_ref, yfl_ref) + _ctx_or_lat(ybc_ref, ybl_ref)
    mu = _seg_sum(y, ones) * (1.0 / HD)
    yc = y - mu
    var = _seg_sum(yc * yc, ones) * (1.0 / HD)
    out = yc * lax.rsqrt(var + RW_GN_EPS) * lg_ref[...] + lb_ref[...] + bonus_ref[...]
    o_ref[...] = x_ref[...] + mod_ref[0][5:6] * _mm(out * g_ref[...], wo_ref[...])


def _rwkv_out_call(y_ctx, y_lat, bonus, g, x, mods_l, lg, lb, wo):
    return pl.pallas_call(
        _rwkv_out_kernel,
        out_shape=jax.ShapeDtypeStruct((N_TOK, D), F32),
        grid=(N_TOK // TM,),
        in_specs=[*_ctx_lat_specs((TM, D)), *_ctx_lat_specs((TM, D)), _tok(D), _tok(D), _tok(D), _modspec(),
                  _full((1, D)), _full((1, D)), _full((D, D))],
        out_specs=_tok(D),
        compiler_params=_params(),
        name="rwkv_out",
    )(y_ctx[0], y_lat[0], y_ctx[1], y_lat[1], bonus, g, x, mods_l, _row(lg), _row(lb), wo)


def _rope_tables():
    t = jnp.arange(DEC_SEQ)
    rowp = (t // GRID_W).astype(F32)
    colp = (t % GRID_W).astype(F32)
    axis_dim = HD // 2
    freqs = ROPE_THETA ** (-jnp.arange(0, axis_dim, 2, dtype=F32) / axis_dim)
    ang = jnp.concatenate([rowp[:, None] * freqs, colp[:, None] * freqs], axis=-1)
    cos = jnp.repeat(jnp.cos(ang), 2, axis=-1)
    sin = jnp.repeat(jnp.sin(ang), 2, axis=-1) * jnp.tile(jnp.array([-1.0, 1.0], F32), HD // 2)
    cos = jnp.concatenate([jnp.ones((TM, HD), F32), cos], axis=0)
    sin = jnp.concatenate([jnp.zeros((TM, HD), F32), sin], axis=0)
    return jnp.tile(cos, (1, LANES // HD)), jnp.tile(sin, (1, LANES // HD))


def _block_diag2(w):
    z = jnp.zeros_like(w[0])
    return jnp.concatenate([jnp.concatenate([w[0], z], axis=1), jnp.concatenate([z, w[1]], axis=1)], axis=0)


def kernel(x_prompt, x_sample, c, c_ctx, cache_k0, cache_v0, cache_k2, cache_v2, state_wkv3,
           norm_g, mod_w, mod_b, ffn_w_in, ffn_w_down, final_norm_g,
           gq_w_qkv, gq_q_norm, gq_k_norm, gq_w_o,
           cv_w_in, cv_b_in, cv_w_dw, cv_b_dw, cv_ln_g, cv_ln_b, cv_w_out, cv_b_out,
           df_w_qkv, df_lambda_q1, df_lambda_k1, df_lambda_q2, df_lambda_k2, df_subln_g, df_w_o,
           rw_mix, rw_w_r, rw_w_k, rw_w_v, rw_w_o, rw_k_k, rw_k_a, rw_r_k, rw_g1, rw_g2,
           rw_ln_g, rw_ln_b, rw_w0, rw_w1, rw_w2, rw_a0, rw_a1, rw_a2):
    bf = lambda w: w.astype(BF16)
    cond8 = jnp.concatenate([c_ctx[None, :], c, jnp.zeros((8 - 1 - DEC_BATCH, D), F32)], axis=0)
    mods = _mods_call(cond8, mod_w, mod_b)
    cos_t, sin_t = _rope_tables()

    def ffn(x, layer, which, final_g=None):
        return _ffn_call(x, mods[layer], norm_g[layer, 2 * which], ffn_w_in, ffn_w_down, layer, which, final_g)

    x = ffn((x_prompt.reshape(N_CTX, D), x_sample.reshape(N_LAT, D)), 0, 0)
    q, kc, kl, vc, vl = _gqa_proj_call(x, mods[0], norm_g[0, 1], bf(gq_w_qkv), gq_q_norm, gq_k_norm, cos_t, sin_t)
    new_k0 = kc.reshape(BATCH, SEQ, GQ_KV, HD)
    new_v0 = vc.reshape(BATCH, SEQ, GQ_KV, HD)
    oc = _gqa_attn_call(q, kc, vc)
    ol = _gqa_attn_call(q, kl, vl, cache_k0.reshape(DEC_BATCH * PAST, GQ_KVW),
                        cache_v0.reshape(DEC_BATCH * PAST, GQ_KVW))
    x = _out_call(oc, ol, x, mods[0], bf(gq_w_o))
    x = ffn(x, 0, 1)

    x = ffn(x, 1, 0)
    u = _conv_in_call(x, mods[1], norm_g[1, 1], bf(cv_w_in), cv_b_in)
    x = _conv_out_call(u, x, mods[1], cv_w_dw, cv_b_dw, cv_ln_g, cv_ln_b, bf(cv_w_out), cv_b_out)
    x = ffn(x, 1, 1)

    x = ffn(x, 2, 0)
    q, kc, kl, vc, vl = _diff_proj_call(x, mods[2], norm_g[2, 1], bf(df_w_qkv), cos_t, sin_t)
    new_k2 = kc.reshape(BATCH, SEQ, DF_HEADS, 2, HD)
    new_v2 = vc.reshape(BATCH, SEQ, DF_HEADS, 2 * HD)
    lam_rows = jnp.stack([df_lambda_q1, df_lambda_k1, df_lambda_q2, df_lambda_k2]).astype(F32)
    oc = _diff_attn_call(q, kc, vc, lam_rows, df_subln_g)
    ol = _diff_attn_call(q, kl, vl, lam_rows, df_subln_g, cache_k2.reshape(DEC_BATCH * PAST, D),
                         cache_v2.reshape(DEC_BATCH * PAST, D))
    x = _out_call(oc, ol, x, mods[2], bf(df_w_o))
    x = ffn(x, 2, 1)

    x = ffn(x, 3, 0)
    w1 = bf(jnp.concatenate([rw_w1[0], rw_w1[1]], axis=1))
    a1 = bf(jnp.concatenate([rw_a1[0], rw_a1[1]], axis=1))
    r, v, kn, g, bonus, lw, kd, al = _rwkv_in_call(
        x, mods[3], norm_g[3, 1], rw_mix, bf(rw_w_r), bf(rw_w_k), bf(rw_w_v), bf(rw_g1), bf(rw_g2),
        w1, bf(_block_diag2(rw_w2)), rw_w0.reshape(-1), a1, bf(_block_diag2(rw_a2)), rw_a0.reshape(-1),
        rw_k_k, rw_k_a, rw_r_k.reshape(-1))
    *y_ctx, new_wkv3 = _wkv_call(r, v, kn, lw, kd, al, BATCH, SEQ, 0)
    *y_lat, _ = _wkv_call(r, v, kn, lw, kd, al, DEC_BATCH, DEC_SEQ, N_CTX, state_wkv3)
    x = _rwkv_out_call(y_ctx, y_lat, bonus, g, x, mods[3], rw_ln_g, rw_ln_b, bf(rw_w_o))
    y_prompt, y_sample = ffn(x, 3, 1, final_norm_g)
    y_prompt = y_prompt.reshape(BATCH, SEQ, D)
    y_sample = y_sample.reshape(DEC_BATCH, DEC_SEQ, D)
    return (y_prompt, y_sample, new_k0, new_v0, new_k2, new_v2, new_wkv3)
```
